```python
import math
import jax, jax.numpy as jnp
from jax import lax
import numpy as np

D_MODEL = 2048
BATCH = 4
SEQ = 4096
DEPTH = 2

CHUNK = 64
PLE_DIM = 256
ROPE_THETA = 10000.0
NORM_EPS = 1e-6
Q_BLOCK = 128
NEG_INF = -1e30
F32 = jnp.float32

A_HEADS = 8
A_KV_HEADS = 2
A_HEAD_DIM = 128
IDX_HEADS = 8
IDX_DIM = 64
IDX_SCALE = (IDX_HEADS * IDX_DIM) ** -0.5
TOPK_MAX = 256

B_HEADS = 8
B_HEAD_DIM = 128
CONV_WIDTH = 4

C_HEADS = 8
C_Q_RANK = 512
C_KV_RANK = 512
C_NOPE = 128
C_ROPE = 64
C_V = 128

D_INNER = 1024
D_HEAD_DIM = 64
D_HEADS = D_INNER // D_HEAD_DIM
D_GROUPS = 2
D_STATE = 128

N_GROUPS = 4
EXPERTS_PER_GROUP = 8
N_EXPERTS = N_GROUPS * EXPERTS_PER_GROUP
EXPERT_FF = 256
TOP_K_EXPERT = 2

N_EVEN = (DEPTH + 1) // 2
N_ODD = DEPTH // 2

EVEN_SIZES = (A_HEADS * A_HEAD_DIM, A_KV_HEADS * A_HEAD_DIM, A_KV_HEADS * A_HEAD_DIM,
              IDX_HEADS * IDX_DIM, IDX_DIM, IDX_HEADS,
              3 * B_HEADS * B_HEAD_DIM, B_HEADS * B_HEAD_DIM, B_HEADS, B_HEADS)
EVEN_IN = sum(EVEN_SIZES)
EVEN_OUT = A_HEADS * A_HEAD_DIM + B_HEADS * B_HEAD_DIM
ODD_SIZES = (C_Q_RANK, C_KV_RANK, C_ROPE, D_INNER, D_INNER + 2 * D_GROUPS * D_STATE, D_HEADS)
ODD_IN = sum(ODD_SIZES)
ODD_OUT = C_HEADS * C_V + D_INNER

kernel_name = "hybrid_dsa_gdn_mla_ssd_hmoe_ple"


def split_cols(t, sizes):
    return jnp.split(t, np.cumsum(sizes)[:-1].tolist(), axis=-1)


def rms_norm(x, w):
    xf = x.astype(F32)
    y = xf * lax.rsqrt(jnp.mean(xf * xf, axis=-1, keepdims=True) + NORM_EPS)
    return (y * w.astype(F32)).astype(x.dtype)


def l2_normalize(t):
    tf = t.astype(F32)
    return tf * lax.rsqrt(jnp.sum(tf * tf, axis=-1, keepdims=True) + 1e-6)


def rope_tables(positions, dim):
    inv_freq = jnp.power(ROPE_THETA, -jnp.arange(0, dim, 2, dtype=F32) / dim)
    ang = positions.astype(F32)[:, :, None, None] * inv_freq
    return jnp.cos(ang), jnp.sin(ang)


def apply_rope(x, cos, sin):
    xf = x.astype(F32)
    x1, x2 = jnp.split(xf, 2, axis=-1)
    return jnp.concatenate([x1 * cos - x2 * sin, x2 * cos + x1 * sin], axis=-1).astype(x.dtype)


def causal_dwconv(x, w):
    width, ch = w.shape
    return lax.conv_general_dilated(x, w[:, None, :].astype(x.dtype), window_strides=(1,),
                                    padding=[(width - 1, 0)],
                                    dimension_numbers=('NWC', 'WIO', 'NWC'),
                                    feature_group_count=ch)


def to_query_blocks(t, nb):
    return jnp.moveaxis(t.reshape(t.shape[0], nb, Q_BLOCK, *t.shape[2:]), 1, 0)


def chunk_causal_attention(q, k, v, scale):
    b_, s_, h_, _ = q.shape
    nb = s_ // Q_BLOCK
    key_chunk = jnp.arange(s_) // CHUNK

    def one_block(args):
        qi, bi = args
        q_chunk = (bi * Q_BLOCK + jnp.arange(Q_BLOCK)) // CHUNK
        logits = jnp.einsum('bqhd,bkhd->bhqk', qi, k, preferred_element_type=F32) * scale
        logits = jnp.where((key_chunk[None, :] <= q_chunk[:, None])[None, None], logits, NEG_INF)
        prob = jax.nn.softmax(logits, axis=-1).astype(v.dtype)
        return jnp.einsum('bhqk,bkhd->bqhd', prob, v)

    out = lax.map(one_block, (to_query_blocks(q, nb), jnp.arange(nb)))
    return jnp.moveaxis(out, 0, 1).reshape(b_, s_, h_, v.shape[-1])


def dsa_attention(q, k, v, iq, ik, iw):
    b_, s_, h_, d_ = q.shape
    g_ = k.shape[2]
    r_ = h_ // g_
    topk = min(TOPK_MAX, s_ // 4)
    nb = s_ // Q_BLOCK
    key_chunk = jnp.arange(s_) // CHUNK
    gather = jax.vmap(lambda kv_b, idx_b: kv_b[idx_b])

    def one_block(args):
        qi, iqi, iwi, bi = args
        q_chunk = (bi * Q_BLOCK + jnp.arange(Q_BLOCK)) // CHUNK
        rel = jax.nn.relu(jnp.einsum('bqhd,bsd->bqhs', iqi, ik, preferred_element_type=F32))
        score = jnp.einsum('bqh,bqhs->bqs', iwi.astype(F32), rel) * IDX_SCALE
        admissible = key_chunk[None, :] <= q_chunk[:, None]
        score = jnp.where(admissible[None], score, NEG_INF)
        _, sel = lax.top_k(score, topk)
        valid = key_chunk[sel] <= q_chunk[None, :, None]
        k_sel = gather(k, sel)
        v_sel = gather(v, sel)
        qg = qi.reshape(b_, Q_BLOCK, g_, r_, d_)
        logits = jnp.einsum('bqgrd,bqkgd->bqgrk', qg, k_sel, preferred_element_type=F32) * d_ ** -0.5
        logits = jnp.where(valid[:, :, None, None, :], logits, NEG_INF)
        prob = jax.nn.softmax(logits, axis=-1).astype(v.dtype)
        return jnp.einsum('bqgrk,bqkgd->bqgrd', prob, v_sel).reshape(b_, Q_BLOCK, h_, d_)

    out = lax.map(one_block, (to_query_blocks(q, nb), to_query_blocks(iq, nb),
                              to_query_blocks(iw, nb), jnp.arange(nb)))
    return jnp.moveaxis(out, 0, 1).reshape(b_, s_, h_, d_)


def gated_delta_rule(q, k, v, g, beta):
    b_, s_, h_, dk = q.shape
    dv = v.shape[-1]
    nc = s_ // CHUNK

    def to_chunks(t):
        t = jnp.moveaxis(t.astype(F32), 2, 1)
        return t.reshape(b_, h_, nc, CHUNK, *t.shape[3:])

    q, k, v, g, beta = (to_chunks(t) for t in (q, k, v, g, beta))
    q = q * dk ** -0.5
    gc = jnp.cumsum(g, axis=-1)
    incl = jnp.tril(jnp.ones((CHUNK, CHUNK), bool))
    strict = jnp.tril(jnp.ones((CHUNK, CHUNK), bool), -1)
    eye = jnp.eye(CHUNK, dtype=F32)
    diff = gc[..., :, None] - gc[..., None, :]
    decay = jnp.where(incl, jnp.exp(jnp.where(incl, diff, 0.0)), 0.0)
    k_beta = k * beta[..., None]
    a_mat = jnp.where(strict, jnp.einsum('bhnid,bhnjd->bhnij', k_beta, k) * decay, 0.0)
    t_mat = lax.linalg.triangular_solve(eye + a_mat, jnp.broadcast_to(eye, a_mat.shape),
                                        left_side=True, lower=True, unit_diagonal=True)
    u = t_mat @ (v * beta[..., None])
    w = t_mat @ (k_beta * jnp.exp(gc)[..., None])
    qk = jnp.einsum('bhnid,bhnjd->bhnij', q, k) * decay

    def step(state, xs):
        q_c, k_c, u_c, w_c, g_c, qk_c = xs
        v_new = u_c - w_c @ state
        o = (q_c * jnp.exp(g_c)[..., None]) @ state + qk_c @ v_new
        g_last = g_c[..., -1:]
        state = state * jnp.exp(g_last)[..., None] + jnp.einsum(
            'bhcd,bhce->bhde', k_c * jnp.exp(g_last - g_c)[..., None], v_new)
        return state, o

    xs = tuple(jnp.moveaxis(t, 2, 0) for t in (q, k, u, w, gc, qk))
    _, o = lax.scan(step, jnp.zeros((b_, h_, dk, dv), F32), xs)
    o = jnp.moveaxis(o, 0, 2).reshape(b_, h_, s_, dv)
    return jnp.moveaxis(o, 1, 2)


def ssd_scan(x, dt, a, bm, cm):
    b_, s_, g_, r_, p_ = x.shape
    n_ = bm.shape[-1]
    nc = s_ // CHUNK
    xdt = (x.astype(F32) * dt[..., None]).reshape(b_, nc, CHUNK, g_, r_, p_)
    la = (dt * a).reshape(b_, nc, CHUNK, g_, r_)
    bm = bm.astype(F32).reshape(b_, nc, CHUNK, g_, n_)
    cm = cm.astype(F32).reshape(b_, nc, CHUNK, g_, n_)
    cum = jnp.cumsum(la, axis=2)
    incl = jnp.tril(jnp.ones((CHUNK, CHUNK), bool))[:, :, None, None]
    seg = cum[:, :, :, None] - cum[:, :, None, :]
    lmat = jnp.where(incl, jnp.exp(jnp.where(incl, seg, 0.0)), 0.0)
    cb = jnp.einsum('bclgn,bcsgn->bclsg', cm, bm)
    y_diag = jnp.einsum('bclsgr,bcsgrp->bclgrp', cb[..., None] * lmat, xdt)
    decay_states = jnp.exp(cum[:, :, -1:] - cum)
    states = jnp.einsum('bclgn,bclgrp->bcgrpn', bm, xdt * decay_states[..., None])
    chunk_decay = jnp.exp(cum[:, :, -1])

    def step(h, xs):
        st, dec = xs
        return h * dec[..., None, None] + st, h

    _, h_prev = lax.scan(step, jnp.zeros((b_, g_, r_, p_, n_), F32),
                         (jnp.moveaxis(states, 1, 0), jnp.moveaxis(chunk_decay, 1, 0)))
    h_prev = jnp.moveaxis(h_prev, 0, 1)
    y_off = jnp.einsum('bclgn,bcgrpn->bclgrp', cm, h_prev) * jnp.exp(cum)[..., None]
    return (y_diag + y_off).reshape(b_, s_, g_, r_, p_)


def even_mixer(hn, rope_a, rope_i, w_in, w_out, conv_w, a_log, dt_bias, norm_w):
    b_, s_, _ = hn.shape
    aq, ak, av, iq, ik, iw, qkv, z, b_raw, a_raw = split_cols(hn @ w_in, EVEN_SIZES)
    aq = apply_rope(aq.reshape(b_, s_, A_HEADS, A_HEAD_DIM), *rope_a)
    ak = apply_rope(ak.reshape(b_, s_, A_KV_HEADS, A_HEAD_DIM), *rope_a)
    av = av.reshape(b_, s_, A_KV_HEADS, A_HEAD_DIM)
    iq = apply_rope(iq.reshape(b_, s_, IDX_HEADS, IDX_DIM), *rope_i)
    ik = apply_rope(ik.reshape(b_, s_, 1, IDX_DIM), *rope_i)[:, :, 0]
    o_a = dsa_attention(aq, ak, av, iq, ik, iw)
    qkv = jax.nn.silu(causal_dwconv(qkv, conv_w))
    bq, bk, bv = jnp.split(qkv, 3, axis=-1)
    heads = lambda t: t.reshape(b_, s_, B_HEADS, B_HEAD_DIM)
    beta = jax.nn.sigmoid(b_raw.astype(F32))
    g = -jnp.exp(a_log.astype(F32)) * jax.nn.softplus(a_raw.astype(F32) + dt_bias.astype(F32))
    o_b = gated_delta_rule(l2_normalize(heads(bq)), l2_normalize(heads(bk)), heads(bv), g, beta)
    o_b = rms_norm(o_b.astype(hn.dtype), norm_w) * jax.nn.silu(heads(z))
    mix = jnp.concatenate([o_a.reshape(b_, s_, -1), o_b.reshape(b_, s_, -1)], axis=-1)
    return mix @ w_out


def odd_mixer(hn, rope_r, w_in, w_out, q_norm, kv_norm, w_uq, w_ukv,
              conv_w, conv_b, a_log, dt_bias, d_skip, norm_w):
    b_, s_, _ = hn.shape
    cq, ckv, kr, z, xbc, dt_raw = split_cols(hn @ w_in, ODD_SIZES)
    q = (rms_norm(cq, q_norm) @ w_uq).reshape(b_, s_, C_HEADS, C_NOPE + C_ROPE)
    q_nope, q_rope = jnp.split(q, [C_NOPE], axis=-1)
    kv = (rms_norm(ckv, kv_norm) @ w_ukv).reshape(b_, s_, C_HEADS, C_NOPE + C_V)
    k_nope, v = jnp.split(kv, [C_NOPE], axis=-1)
    q = jnp.concatenate([q_nope, apply_rope(q_rope, *rope_r)], axis=-1)
    k_rope = apply_rope(kr.reshape(b_, s_, 1, C_ROPE), *rope_r)
    k = jnp.concatenate([k_nope, jnp.broadcast_to(k_rope, (b_, s_, C_HEADS, C_ROPE))], axis=-1)
    o_c = chunk_causal_attention(q, k, v, (C_NOPE + C_ROPE) ** -0.5)
    xbc = jax.nn.silu(causal_dwconv(xbc, conv_w) + conv_b)
    xs, bm, cm = split_cols(xbc, (D_INNER, D_GROUPS * D_STATE, D_GROUPS * D_STATE))
    r_ = D_HEADS // D_GROUPS
    xs = xs.reshape(b_, s_, D_GROUPS, r_, D_HEAD_DIM)
    bm = bm.reshape(b_, s_, D_GROUPS, D_STATE)
    cm = cm.reshape(b_, s_, D_GROUPS, D_STATE)
    dt = jax.nn.softplus(dt_raw.astype(F32) + dt_bias.astype(F32)).reshape(b_, s_, D_GROUPS, r_)
    a = -jnp.exp(a_log.astype(F32)).reshape(D_GROUPS, r_)
    y = ssd_scan(xs, dt, a, bm, cm)
    y = y + d_skip.astype(F32).reshape(D_GROUPS, r_)[..., None] * xs.astype(F32)
    y = y.reshape(b_, s_, D_INNER).astype(hn.dtype) * jax.nn.silu(z)
    y = rms_norm(y.reshape(b_, s_, D_GROUPS, -1), norm_w.reshape(D_GROUPS, -1)).reshape(b_, s_, D_INNER)
    mix = jnp.concatenate([o_c.reshape(b_, s_, -1), y], axis=-1)
    return mix @ w_out


def hier_moe(xn, w_group, b_group, w_expert, b_expert, w_gate, w_up, w_down):
    b_, s_, d_ = xn.shape
    t = xn.reshape(b_ * s_, d_)
    g_logit = (t @ w_group).astype(F32) + b_group.astype(F32)
    g_prob = jax.nn.softmax(g_logit, axis=-1)
    g_sel = jnp.argmax(g_logit, axis=-1)
    e_logit = ((t @ w_expert).astype(F32) + b_expert.astype(F32)).reshape(-1, N_GROUPS, EXPERTS_PER_GROUP)
    e_logit = jnp.take_along_axis(e_logit, g_sel[:, None, None], axis=1)[:, 0]
    top_val, top_idx = lax.top_k(e_logit, TOP_K_EXPERT)
    weight = jax.nn.softmax(top_val, axis=-1) * jnp.take_along_axis(g_prob, g_sel[:, None], axis=1)
    expert_id = g_sel[:, None] * EXPERTS_PER_GROUP + top_idx
    combine = jnp.einsum('tk,tke->te', weight, jax.nn.one_hot(expert_id, N_EXPERTS, dtype=F32))
    hid = jax.nn.silu(jnp.einsum('td,edf->tef', t, w_gate)) * jnp.einsum('td,edf->tef', t, w_up)
    out = jnp.einsum('tef,efd->td', hid * combine[..., None].astype(hid.dtype), w_down)
    return out.reshape(b_, s_, d_)


def setup_inputs(seed: int = 0) -> dict:
    key = jax.random.key(seed)
    ks = iter(jax.random.split(key, 48))
    nrm = lambda shape, fan_in: jax.random.normal(next(ks), shape, F32) * fan_in ** -0.5
    gain = lambda shape: 1.0 + 0.05 * jax.random.normal(next(ks), shape, F32)
    small = lambda shape: 0.01 * jax.random.normal(next(ks), shape, F32)

    def log_uniform(shape, lo, hi):
        u = jax.random.uniform(next(ks), shape, F32)
        return jnp.exp(math.log(lo) + u * (math.log(hi) - math.log(lo)))

    def dt_bias(shape):
        dt = log_uniform(shape, 1e-3, 1e-1)
        return dt + jnp.log(-jnp.expm1(-dt))

    x = jax.random.normal(next(ks), (BATCH, SEQ, D_MODEL), F32)
    p = jax.random.normal(next(ks), (DEPTH, BATCH, SEQ, PLE_DIM), F32)
    positions = (jax.random.randint(next(ks), (BATCH, 1), 0, 4096, jnp.int32)
                 + jnp.arange(SEQ, dtype=jnp.int32)[None, :])
    return {
        "x": x, "p": p, "positions": positions,
        "norm_mix": gain((DEPTH, D_MODEL)),
        "norm_ffn": gain((DEPTH, D_MODEL)),
        "norm_ple": gain((DEPTH, D_MODEL)),
        "norm_final": gain((D_MODEL,)),
        "ev_w_in": nrm((N_EVEN, D_MODEL, EVEN_IN), D_MODEL),
        "ev_w_out": nrm((N_EVEN, EVEN_OUT, D_MODEL), EVEN_OUT),
        "gdn_conv_w": nrm((N_EVEN, CONV_WIDTH, 3 * B_HEADS * B_HEAD_DIM), CONV_WIDTH),
        "gdn_a_log": jnp.log(jax.random.uniform(next(ks), (N_EVEN, B_HEADS), F32, 1.0, 16.0)),
        "gdn_dt_bias": dt_bias((N_EVEN, B_HEADS)),
        "gdn_norm": gain((N_EVEN, B_HEAD_DIM)),
        "od_w_in": nrm((N_ODD, D_MODEL, ODD_IN), D_MODEL),
        "od_w_out": nrm((N_ODD, ODD_OUT, D_MODEL), ODD_OUT),
        "mla_q_norm": gain((N_ODD, C_Q_RANK)),
        "mla_kv_norm": gain((N_ODD, C_KV_RANK)),
        "mla_w_uq": nrm((N_ODD, C_Q_RANK, C_HEADS * (C_NOPE + C_ROPE)), C_Q_RANK),
        "mla_w_ukv": nrm((N_ODD, C_KV_RANK, C_HEADS * (C_NOPE + C_V)), C_KV_RANK),
        "ssm_conv_w": nrm((N_ODD, CONV_WIDTH, D_INNER + 2 * D_GROUPS * D_STATE), CONV_WIDTH),
        "ssm_conv_b": small((N_ODD, D_INNER + 2 * D_GROUPS * D_STATE)),
        "ssm_a_log": jnp.log(jax.random.uniform(next(ks), (N_ODD, D_HEADS), F32, 1.0, 16.0)),
        "ssm_dt_bias": dt_bias((N_ODD, D_HEADS)),
        "ssm_d_skip": gain((N_ODD, D_HEADS)),
        "ssm_norm": gain((N_ODD, D_INNER)),
        "moe_w_group": nrm((DEPTH, D_MODEL, N_GROUPS), D_MODEL),
        "moe_b_group": small((DEPTH, N_GROUPS)),
        "moe_w_expert": nrm((DEPTH, D_MODEL, N_EXPERTS), D_MODEL),
        "moe_b_expert": small((DEPTH, N_EXPERTS)),
        "moe_w_gate": nrm((DEPTH, N_EXPERTS, D_MODEL, EXPERT_FF), D_MODEL),
        "moe_w_up": nrm((DEPTH, N_EXPERTS, D_MODEL, EXPERT_FF), D_MODEL),
        "moe_w_down": nrm((DEPTH, N_EXPERTS, EXPERT_FF, D_MODEL), EXPERT_FF),
        "ple_w_proj": nrm((DEPTH, PLE_DIM, D_MODEL), PLE_DIM),
        "ple_w_gate": nrm((DEPTH, D_MODEL, D_MODEL), D_MODEL),
    }


def reference(x, p, positions, norm_mix, norm_ffn, norm_ple, norm_final,
              ev_w_in, ev_w_out, gdn_conv_w, gdn_a_log, gdn_dt_bias, gdn_norm,
              od_w_in, od_w_out, mla_q_norm, mla_kv_norm, mla_w_uq, mla_w_ukv,
              ssm_conv_w, ssm_conv_b, ssm_a_log, ssm_dt_bias, ssm_d_skip, ssm_norm,
              moe_w_group, moe_b_group, moe_w_expert, moe_b_expert,
              moe_w_gate, moe_w_up, moe_w_down, ple_w_proj, ple_w_gate):
    rope_a = rope_tables(positions, A_HEAD_DIM)
    rope_i = rope_tables(positions, IDX_DIM)
    rope_r = rope_tables(positions, C_ROPE)
    h = x
    for i in range(DEPTH):
        j = i // 2
        hn = rms_norm(h, norm_mix[i])
        if i % 2 == 0:
            mix = even_mixer(hn, rope_a, rope_i, ev_w_in[j], ev_w_out[j], gdn_conv_w[j],
                             gdn_a_log[j], gdn_dt_bias[j], gdn_norm[j])
        else:
            mix = odd_mixer(hn, rope_r, od_w_in[j], od_w_out[j], mla_q_norm[j], mla_kv_norm[j],
                            mla_w_uq[j], mla_w_ukv[j], ssm_conv_w[j], ssm_conv_b[j],
                            ssm_a_log[j], ssm_dt_bias[j], ssm_d_skip[j], ssm_norm[j])
        h = h + mix
        h = h + hier_moe(rms_norm(h, norm_ffn[i]), moe_w_group[i], moe_b_group[i],
                         moe_w_expert[i], moe_b_expert[i], moe_w_gate[i], moe_w_up[i], moe_w_down[i])
        gate = jax.nn.sigmoid(rms_norm(h, norm_ple[i]) @ ple_w_gate[i])
        h = h + gate * (p[i] @ ple_w_proj[i])
    return rms_norm(h, norm_final)
```

```python
import functools
import math

import jax
import jax.numpy as jnp
from jax import lax
from jax.experimental import pallas as pl
from jax.experimental.pallas import tpu as pltpu

F32 = jnp.float32
BF16 = jnp.bfloat16
I32 = jnp.int32

CHUNK = 64
ROPE_THETA = 10000.0
NORM_EPS = 1e-6
NEG_INF = -1e30
BISECT_STEPS = 24

A_HEADS, A_KV_HEADS, A_HEAD_DIM = 8, 2, 128
IDX_HEADS, IDX_DIM = 8, 64
IDX_SCALE = (IDX_HEADS * IDX_DIM) ** -0.5
TOPK_MAX = 256
B_HEADS, B_HEAD_DIM, CONV_WIDTH = 8, 128, 4
C_HEADS, C_Q_RANK, C_KV_RANK, C_NOPE, C_ROPE, C_V = 8, 512, 512, 128, 64, 128
D_INNER, D_HEAD_DIM, D_GROUPS, D_STATE = 1024, 64, 2, 128
D_HEADS = D_INNER // D_HEAD_DIM
N_GROUPS, EXPERTS_PER_GROUP, EXPERT_FF = 4, 8, 256
N_EXPERTS = N_GROUPS * EXPERTS_PER_GROUP

LANE = 128
VMEM_BYTES_V7X = 64 * 1024 * 1024
VMEM_LIMIT = VMEM_BYTES_V7X - 8 * 1024 * 1024


def _cparams(*sem):
    return pltpu.CompilerParams(dimension_semantics=sem, vmem_limit_bytes=VMEM_LIMIT)


def _dot(a, b):
    return jnp.dot(a, b, preferred_element_type=F32)


def _dot_nt(a, b):
    return lax.dot_general(a, b, (((1,), (1,)), ((), ())), preferred_element_type=F32)


def _bdot(a, b):
    return _dot(a.astype(BF16), b.astype(BF16))


def _bdot_nt(a, b):
    return _dot_nt(a.astype(BF16), b.astype(BF16))


def _split3(x):
    hi = x.astype(BF16)
    r1 = x - hi.astype(F32)
    mid = r1.astype(BF16)
    lo = (r1 - mid.astype(F32)).astype(BF16)
    return hi, mid, lo


def _dot01_left(m01, x):
    hi, mid, lo = _split3(x)
    return _dot(m01, hi) + _dot(m01, mid) + _dot(m01, lo)


def _dot01_right(x, m01):
    hi, mid, lo = _split3(x)
    return _dot(hi, m01) + _dot(mid, m01) + _dot(lo, m01)


def _sigmoid(x):
    return 1.0 / (1.0 + jnp.exp(-x))


def _silu(x):
    return x * _sigmoid(x)


def _softplus(x):
    return jnp.maximum(x, 0.0) + jnp.log1p(jnp.exp(-jnp.abs(x)))


def _rope64(x, c, slo, shi):
    return x * c + pltpu.roll(x, 96, 1) * slo + pltpu.roll(x, 32, 1) * shi


def _rope128(x, c, s):
    return x * c + pltpu.roll(x, 64, 1) * s


def _norm_matmul_kernel(x_ref, g_ref, w_ref, o_ref, xn_ref):
    @pl.when(pl.program_id(1) == 0)
    def _():
        x = x_ref[...].astype(F32)
        ms = jnp.mean(x * x, axis=-1, keepdims=True)
        xn_ref[...] = (x * lax.rsqrt(ms + NORM_EPS) * g_ref[...]).astype(BF16)

    o_ref[...] = _dot(xn_ref[...], w_ref[...]).astype(o_ref.dtype)


def norm_matmul(x, g, w, *, k, xblk=0, tm=512, tn=None, out_dtype=F32):
    t = x.shape[0]
    n = w.shape[1]
    tn = n if tn is None else tn
    tm = min(tm, t)
    return pl.pallas_call(
        _norm_matmul_kernel,
        grid=(t // tm, n // tn),
        in_specs=[pl.BlockSpec((tm, k), lambda i, j: (i, xblk)),
                  pl.BlockSpec((1, k), lambda i, j: (0, 0)),
                  pl.BlockSpec((k, tn), lambda i, j: (0, j))],
        out_specs=pl.BlockSpec((tm, tn), lambda i, j: (i, j)),
        out_shape=jax.ShapeDtypeStruct((t, n), out_dtype),
        scratch_shapes=[pltpu.VMEM((tm, k), BF16)],
        compiler_params=_cparams("parallel", "arbitrary"),
        name="norm_matmul",
    )(x, g.reshape(1, k).astype(F32), w)


def _outproj_kernel(a1_ref, a2_ref, w1_ref, w2_ref, r_ref, o_ref):
    o_ref[...] = r_ref[...] + _dot(a1_ref[...], w1_ref[...]) + _dot(a2_ref[...], w2_ref[...])


def outproj_residual(a1, a2, w, res, *, tm=512, tn=1024):
    t, kh = a1.shape
    n = w.shape[1]
    tm = min(tm, t)
    return pl.pallas_call(
        _outproj_kernel,
        grid=(t // tm, n // tn),
        in_specs=[pl.BlockSpec((tm, kh), lambda i, j: (i, 0)),
                  pl.BlockSpec((tm, kh), lambda i, j: (i, 0)),
                  pl.BlockSpec((kh, tn), lambda i, j: (0, j)),
                  pl.BlockSpec((kh, tn), lambda i, j: (1, j)),
                  pl.BlockSpec((tm, tn), lambda i, j: (i, j))],
        out_specs=pl.BlockSpec((tm, tn), lambda i, j: (i, j)),
        out_shape=jax.ShapeDtypeStruct((t, n), F32),
        compiler_params=_cparams("parallel", "arbitrary"),
        name="outproj",
    )(a1, a2, w, w, res)


def _ple_kernel(h_ref, g_ref, wg_ref, p_ref, wp_ref, hres_ref, o_ref, xn_ref):
    @pl.when(pl.program_id(1) == 0)
    def _():
        x = h_ref[...]
        ms = jnp.mean(x * x, axis=-1, keepdims=True)
        xn_ref[...] = (x * lax.rsqrt(ms + NORM_EPS) * g_ref[...]).astype(BF16)

    gate = _sigmoid(_dot(xn_ref[...], wg_ref[...]))
    emb = _dot(p_ref[...].astype(BF16), wp_ref[...])
    o_ref[...] = hres_ref[...] + gate * emb


def ple_residual(h, g, wg, p, wp, *, tm=512, tn=1024):
    t, d = h.shape
    pd = p.shape[1]
    tm = min(tm, t)
    return pl.pallas_call(
        _ple_kernel,
        grid=(t // tm, d // tn),
        in_specs=[pl.BlockSpec((tm, d), lambda i, j: (i, 0)),
                  pl.BlockSpec((1, d), lambda i, j: (0, 0)),
                  pl.BlockSpec((d, tn), lambda i, j: (0, j)),
                  pl.BlockSpec((tm, pd), lambda i, j: (i, 0)),
                  pl.BlockSpec((pd, tn), lambda i, j: (0, j)),
                  pl.BlockSpec((tm, tn), lambda i, j: (i, j))],
        out_specs=pl.BlockSpec((tm, tn), lambda i, j: (i, j)),
        out_shape=jax.ShapeDtypeStruct((t, d), F32),
        scratch_shapes=[pltpu.VMEM((tm, d), BF16)],
        compiler_params=_cparams("parallel", "arbitrary"),
        name="ple",
    )(h, g.reshape(1, d).astype(F32), wg, p, wp, h)


def _rmsnorm_kernel(x_ref, g_ref, o_ref):
    x = x_ref[...]
    ms = jnp.mean(x * x, axis=-1, keepdims=True)
    o_ref[...] = x * lax.rsqrt(ms + NORM_EPS) * g_ref[...]


def rmsnorm(x, g, *, tm=512):
    t, d = x.shape
    tm = min(tm, t)
    return pl.pallas_call(
        _rmsnorm_kernel,
        grid=(t // tm,),
        in_specs=[pl.BlockSpec((tm, d), lambda i: (i, 0)),
                  pl.BlockSpec((1, d), lambda i: (0, 0))],
        out_specs=pl.BlockSpec((tm, d), lambda i: (i, 0)),
        out_shape=jax.ShapeDtypeStruct((t, d), F32),
        compiler_params=_cparams("parallel"),
        name="final_norm",
    )(x, g.reshape(1, d).astype(F32))


def _moe_kernel(h_ref, g_ref, wr_ref, br_ref, wgu_ref, wd_ref, o_ref, xn_ref, comb_ref, *, ke):
    e = pl.program_id(1)
    tm = h_ref.shape[0]

    @pl.when(e == 0)
    def _():
        x = h_ref[...]
        ms = jnp.mean(x * x, axis=-1, keepdims=True)
        xn = x * lax.rsqrt(ms + NORM_EPS) * g_ref[...]
        xn_ref[...] = xn.astype(BF16)
        o_ref[...] = x
        logits = jnp.dot(xn, wr_ref[...], preferred_element_type=F32,
                         precision=lax.Precision.HIGHEST) + br_ref[...]
        glog = logits[:, :LANE]
        elog = logits[:, LANE:]
        lane = lax.broadcasted_iota(I32, (tm, LANE), 1)
        glog = jnp.where(lane < N_GROUPS, glog, NEG_INF)
        gmax = jnp.max(glog, axis=-1, keepdims=True)
        g_sel = jnp.min(jnp.where(glog == gmax, lane, LANE), axis=-1, keepdims=True)
        p_sel = 1.0 / jnp.sum(jnp.exp(glog - gmax), axis=-1, keepdims=True)
        in_grp = (lane // EXPERTS_PER_GROUP) == g_sel
        e1 = jnp.where(in_grp, elog, NEG_INF)
        v1 = jnp.max(e1, axis=-1, keepdims=True)
        i1 = jnp.min(jnp.where(e1 == v1, lane, LANE), axis=-1, keepdims=True)
        e2 = jnp.where(lane == i1, NEG_INF, e1)
        v2 = jnp.max(e2, axis=-1, keepdims=True)
        i2 = jnp.min(jnp.where(e2 == v2, lane, LANE), axis=-1, keepdims=True)
        ex = jnp.exp(v2 - v1)
        w1 = p_sel / (1.0 + ex)
        w2 = p_sel * ex / (1.0 + ex)
        comb_ref[...] = jnp.where(lane == i1, w1, 0.0) + jnp.where(lane == i2, w2, 0.0)

    xn = xn_ref[...]
    lane = lax.broadcasted_iota(I32, (tm, LANE), 1)
    comb = comb_ref[...]
    hids = []
    for j in range(ke):
        gu = _dot(xn, wgu_ref[j])
        c = jnp.sum(jnp.where(lane == e * ke + j, comb, 0.0), axis=-1, keepdims=True)
        hid = _silu(gu[:, :EXPERT_FF]) * gu[:, EXPERT_FF:] * c
        hids.append(hid.astype(BF16))
    hid = jnp.concatenate(hids, axis=1)
    wd = wd_ref[...].reshape(ke * EXPERT_FF, wd_ref.shape[2])
    o_ref[...] += _dot(hid, wd)


def moe_residual(h, g, w_router, b_router, w_gu, w_down, *, tm=512, ke=4):
    t, d = h.shape
    tm = min(tm, t)
    return pl.pallas_call(
        functools.partial(_moe_kernel, ke=ke),
        grid=(t // tm, N_EXPERTS // ke),
        in_specs=[pl.BlockSpec((tm, d), lambda i, e: (i, 0)),
                  pl.BlockSpec((1, d), lambda i, e: (0, 0)),
                  pl.BlockSpec((d, 2 * LANE), lambda i, e: (0, 0)),
                  pl.BlockSpec((1, 2 * LANE), lambda i, e: (0, 0)),
                  pl.BlockSpec((ke, d, 2 * EXPERT_FF), lambda i, e: (e, 0, 0)),
                  pl.BlockSpec((ke, EXPERT_FF, d), lambda i, e: (e, 0, 0))],
        out_specs=pl.BlockSpec((tm, d), lambda i, e: (i, 0)),
        out_shape=jax.ShapeDtypeStruct((t, d), F32),
        scratch_shapes=[pltpu.VMEM((tm, d), BF16), pltpu.VMEM((tm, LANE), F32)],
        compiler_params=_cparams("parallel", "arbitrary"),
        name="moe",
    )(h, g.reshape(1, d).astype(F32), w_router, b_router, w_gu, w_down)


def _dsa_prep_kernel(x_ref, ca_ref, sa_ref, ci_ref, slo_ref, shi_ref,
                     q_ref, k_ref, v_ref, iq_ref, ik_ref, iw_ref):
    ca, sa = ca_ref[...], sa_ref[...]
    ci, slo, shi = ci_ref[...], slo_ref[...], shi_ref[...]
    scale = A_HEAD_DIM ** -0.5
    for h in range(A_HEADS):
        xh = x_ref[:, h * LANE:(h + 1) * LANE]
        q_ref[:, h * LANE:(h + 1) * LANE] = (_rope128(xh, ca, sa) * scale).astype(BF16)
    off = A_HEADS * A_HEAD_DIM
    for h in range(A_KV_HEADS):
        xh = x_ref[:, off + h * LANE:off + (h + 1) * LANE]
        k_ref[:, h * LANE:(h + 1) * LANE] = _rope128(xh, ca, sa).astype(BF16)
    off += A_KV_HEADS * A_HEAD_DIM
    v_ref[...] = x_ref[:, off:off + A_KV_HEADS * A_HEAD_DIM].astype(BF16)
    off += A_KV_HEADS * A_HEAD_DIM
    for h in range(IDX_HEADS * IDX_DIM // LANE):
        xh = x_ref[:, off + h * LANE:off + (h + 1) * LANE]
        iq_ref[:, h * LANE:(h + 1) * LANE] = _rope64(xh, ci, slo, shi).astype(BF16)
    off += IDX_HEADS * IDX_DIM
    misc = x_ref[:, off:off + LANE]
    ik_ref[...] = _rope64(misc, ci, slo, shi)[:, :IDX_DIM].astype(BF16)
    iw_ref[...] = misc[:, IDX_DIM:IDX_DIM + IDX_HEADS] * IDX_SCALE


def dsa_prep(proj, tabs, *, tm=512):
    t, n = proj.shape
    tm = min(tm, t)
    row = lambda w: pl.BlockSpec((tm, w), lambda i: (i, 0))
    return pl.pallas_call(
        _dsa_prep_kernel,
        grid=(t // tm,),
        in_specs=[row(n)] + [row(LANE)] * 5,
        out_specs=[row(1024), row(256), row(256), row(512), row(IDX_DIM), row(IDX_HEADS)],
        out_shape=[jax.ShapeDtypeStruct((t, 1024), BF16), jax.ShapeDtypeStruct((t, 256), BF16),
                   jax.ShapeDtypeStruct((t, 256), BF16), jax.ShapeDtypeStruct((t, 512), BF16),
                   jax.ShapeDtypeStruct((t, IDX_DIM), BF16), jax.ShapeDtypeStruct((t, IDX_HEADS), F32)],
        compiler_params=_cparams("parallel"),
        name="dsa_prep",
    )(proj, *tabs)


def _dsa_kernel(q_ref, k_ref, v_ref, iq_ref, ik_ref, iw_ref, o_ref,
                key_ref, m_ref, l_ref, acc_ref, *, tk, topk):
    qi = pl.program_id(1)
    nq = q_ref.shape[0]
    rep = A_HEADS // A_KV_HEADS
    n_tiles = ((qi + 1) * nq + tk - 1) // tk
    row = lax.broadcasted_iota(I32, (nq, tk), 0)
    col = lax.broadcasted_iota(I32, (nq, tk), 1)
    q_chunk = (qi * nq + row) // CHUNK

    def admissible(off):
        return ((off + col) // CHUNK) <= q_chunk

    iq = iq_ref[...]
    iw = iw_ref[...]

    def score_body(t, carry):
        off = pl.multiple_of(t * tk, tk)
        ikt = ik_ref[pl.ds(off, tk), :]
        s = jnp.zeros((nq, tk), F32)
        for h in range(IDX_HEADS):
            rel = _dot_nt(iq[:, h * IDX_DIM:(h + 1) * IDX_DIM], ikt)
            s = s + iw[:, h:h + 1] * jnp.maximum(rel, 0.0)
        key_ref[:, pl.ds(off, tk)] = jnp.where(admissible(off), s, NEG_INF)
        return carry

    lax.fori_loop(0, n_tiles, score_body, 0)

    def scan(fn, init):
        def body(t, acc):
            off = pl.multiple_of(t * tk, tk)
            blk = key_ref[:, pl.ds(off, tk)]
            for c in range(tk // LANE):
                acc = fn(blk[:, c * LANE:(c + 1) * LANE], acc)
            return acc
        return lax.fori_loop(0, n_tiles, body, init)

    def rowsum(x):
        return jnp.sum(x, axis=1, keepdims=True)

    def count_ge(v):
        return rowsum(scan(lambda blk, acc: acc + jnp.where(blk >= v, 1.0, 0.0), jnp.zeros((nq, LANE), F32)))

    big = jnp.full((nq, LANE), -NEG_INF, F32)
    zeros = jnp.zeros((nq, LANE), F32)
    lo_p, hi_p, n_p = scan(
        lambda blk, acc: (jnp.minimum(acc[0], jnp.where(blk > NEG_INF, blk, big)), jnp.maximum(acc[1], blk),
                          acc[2] + jnp.where(blk > NEG_INF, 1.0, 0.0)), (big, -big, zeros))
    enough = rowsum(n_p) >= topk
    lo0 = jnp.where(enough, jnp.min(lo_p, axis=1, keepdims=True), NEG_INF)
    hi0 = jnp.where(enough, jnp.max(hi_p, axis=1, keepdims=True), NEG_INF)

    def bisect_body(i, bracket):
        lo, hi = bracket
        mid = lo + 0.5 * (hi - lo)
        ge = count_ge(mid) >= topk
        return jnp.where(ge, mid, lo), jnp.where(ge, hi, mid)

    _, hi = lax.fori_loop(0, BISECT_STEPS, bisect_body, (lo0, hi0))

    def walk_cond(state):
        return state[1] > 0

    def walk_body(state):
        v, _ = state
        zero = jnp.zeros((nq, LANE), F32)
        cnt_p, nxt_p = scan(lambda blk, acc: (acc[0] + jnp.where(blk >= v, 1.0, 0.0),
                                              jnp.maximum(acc[1], jnp.where(blk < v, blk, -big))),
                            (zero, -big))
        found = rowsum(cnt_p) >= topk
        v = jnp.where(found, v, jnp.max(nxt_p, axis=1, keepdims=True))
        return v, jnp.sum(jnp.where(found, 0, 1))

    hi_val = jnp.max(scan(lambda blk, acc: jnp.maximum(acc, jnp.where(blk <= hi, blk, -big)), -big),
                     axis=1, keepdims=True)
    thr, _ = lax.while_loop(walk_cond, walk_body, (hi_val, jnp.int32(1)))
    n_gt = rowsum(scan(lambda blk, acc: acc + jnp.where(blk > thr, 1.0, 0.0), jnp.zeros((nq, LANE), F32)))
    need = topk - n_gt

    m_ref[...] = jnp.full(m_ref.shape, NEG_INF, F32)
    l_ref[...] = jnp.zeros(l_ref.shape, F32)
    acc_ref[...] = jnp.zeros(acc_ref.shape, F32)
    upper = (lax.broadcasted_iota(I32, (LANE, LANE), 0)
             < lax.broadcasted_iota(I32, (LANE, LANE), 1)).astype(BF16)
    qg = [jnp.concatenate([q_ref[:, (g * rep + r) * LANE:(g * rep + r + 1) * LANE]
                           for r in range(rep)], axis=0) for g in range(A_KV_HEADS)]

    def attn_body(t, run):
        off = pl.multiple_of(t * tk, tk)
        keyt = key_ref[:, pl.ds(off, tk)]
        adm = admissible(off)
        sels = []
        for c in range(tk // LANE):
            kc = keyt[:, c * LANE:(c + 1) * LANE]
            eqf = jnp.where(kc == thr, 1.0, 0.0)
            before = _dot(eqf.astype(BF16), upper) + run
            take = jnp.where(kc > thr, 1.0, jnp.where(before < need, eqf, 0.0))
            sels.append(take)
            run = run + jnp.sum(eqf, axis=1, keepdims=True)
        sel = jnp.where(adm, jnp.concatenate(sels, axis=1), 0.0) > 0.5
        sel_r = jnp.concatenate([sel] * rep, axis=0)
        groups = range(A_KV_HEADS)
        ss = [jnp.where(sel_r, _dot_nt(qg[g], k_ref[pl.ds(off, tk), g * LANE:(g + 1) * LANE]), NEG_INF)
              for g in groups]
        m_old = [m_ref[g] for g in groups]
        m_new = [jnp.maximum(m_old[g], jnp.max(ss[g], axis=1, keepdims=True)) for g in groups]
        ps = [jnp.exp(ss[g] - m_new[g]) for g in groups]
        pv = [_dot(ps[g].astype(BF16), v_ref[pl.ds(off, tk), g * LANE:(g + 1) * LANE]) for g in groups]
        for g in groups:
            alpha = jnp.exp(m_old[g] - m_new[g])
            l_ref[g] = alpha * l_ref[g] + jnp.sum(ps[g], axis=1, keepdims=True)
            acc_ref[g] = alpha * acc_ref[g] + pv[g]
            m_ref[g] = m_new[g]
        return run

    lax.fori_loop(0, n_tiles, attn_body, jnp.zeros((nq, 1), F32))
    for g in range(A_KV_HEADS):
        out = acc_ref[g] / l_ref[g]
        for r in range(rep):
            h = g * rep + r
            o_ref[:, h * LANE:(h + 1) * LANE] = out[r * nq:(r + 1) * nq].astype(o_ref.dtype)


def dsa_attention(q, k, v, iq, ik, iw, *, nq=128, tk=512):
    b, s, _ = q.shape
    tk = min(tk, s)
    topk = min(TOPK_MAX, s // 4)
    rep = A_HEADS // A_KV_HEADS
    qspec = lambda w: pl.BlockSpec((None, nq, w), lambda bi, qi: (bi, qi, 0))
    kspec = lambda w: pl.BlockSpec((None, s, w), lambda bi, qi: (bi, 0, 0))
    return pl.pallas_call(
        functools.partial(_dsa_kernel, tk=tk, topk=topk),
        grid=(b, s // nq),
        in_specs=[qspec(1024), kspec(256), kspec(256), qspec(512), kspec(IDX_DIM), qspec(IDX_HEADS)],
        out_specs=qspec(1024),
        out_shape=jax.ShapeDtypeStruct((b, s, 1024), BF16),
        scratch_shapes=[pltpu.VMEM((nq, s), F32),
                        pltpu.VMEM((A_KV_HEADS, rep * nq, 1), F32),
                        pltpu.VMEM((A_KV_HEADS, rep * nq, 1), F32),
                        pltpu.VMEM((A_KV_HEADS, rep * nq, LANE), F32)],
        compiler_params=_cparams("parallel", "arbitrary"),
        name="dsa_attention",
    )(q, k, v, iq, ik, iw)


def _causal_conv(x, tail_ref, xp_ref, w):
    r = x.shape[0]
    width = w.shape[0]
    xp_ref[0:8, :] = tail_ref[...]
    xp_ref[8:8 + r, :] = x
    tail_ref[...] = x[r - 8:r, :]
    acc = x * w[width - 1:width, :]
    for j in range(width - 1):
        acc = acc + xp_ref[pl.ds(8 - (width - 1) + j, r), :] * w[j:j + 1, :]
    return acc


def _unit_lower_inverse(mats):
    n = mats[0].shape[0]
    eye = (lax.broadcasted_iota(I32, (n, n), 0) == lax.broadcasted_iota(I32, (n, n), 1)).astype(F32)
    ps = [-a for a in mats]
    ts = [eye + p for p in ps]
    for _ in range(int(math.ceil(math.log2(n))) - 1):
        ps = [_bdot(p, p) for p in ps]
        ts = [t + _bdot(t, p) for t, p in zip(ts, ps)]
    return ts


def _gdn_kernel(qkv_ref, z_ref, misc_ref, convw_ref, alog_ref, dtb_ref, nw_ref, o_ref,
                tail_ref, xp_ref, q_s, k_s, v_s, beta_s, g_s, state_ref):
    rows = qkv_ref.shape[0]
    hd = B_HEAD_DIM
    nh = B_HEADS

    @pl.when(pl.program_id(1) == 0)
    def _():
        tail_ref[...] = jnp.zeros(tail_ref.shape, F32)
        state_ref[...] = jnp.zeros(state_ref.shape, F32)

    y = _silu(_causal_conv(qkv_ref[...], tail_ref, xp_ref, convw_ref[...]))
    for h in range(nh):
        qh = y[:, h * hd:(h + 1) * hd]
        kh = y[:, (nh + h) * hd:(nh + h + 1) * hd]
        q_s[:, h * hd:(h + 1) * hd] = qh * lax.rsqrt(jnp.sum(qh * qh, -1, keepdims=True) + 1e-6) * hd ** -0.5
        k_s[:, h * hd:(h + 1) * hd] = kh * lax.rsqrt(jnp.sum(kh * kh, -1, keepdims=True) + 1e-6)
    v_s[...] = y[:, 2 * nh * hd:]
    misc = misc_ref[...]
    beta_s[...] = _sigmoid(misc)
    g_s[...] = -jnp.exp(alog_ref[...]) * _softplus(misc + dtb_ref[...])

    ri = lax.broadcasted_iota(I32, (CHUNK, CHUNK), 0)
    ci = lax.broadcasted_iota(I32, (CHUNK, CHUNK), 1)
    incl = ri >= ci
    strict = ri > ci
    lower01 = incl.astype(BF16)
    nw = nw_ref[...]

    def chunk_body(c, carry):
        r0 = pl.multiple_of(c * CHUNK, CHUNK)
        rs = pl.ds(r0, CHUNK)
        gc = _dot01_left(lower01, g_s[rs, :])
        gct = gc.T
        beta = beta_s[rs, :]
        eg = jnp.exp(gc)
        g_last = gc[CHUNK - 1:CHUNK, :]
        e_last = jnp.exp(g_last)
        e_rem = jnp.exp(g_last - gc)
        heads = range(nh)
        hs = [slice(h * hd, (h + 1) * hd) for h in heads]
        col = lambda x, h: x[:, nh + h:nh + h + 1]
        qh = [q_s[rs, hs[h]] for h in heads]
        kh = [k_s[rs, hs[h]] for h in heads]
        bcol = [beta[:, h:h + 1] for h in heads]
        kb = [kh[h] * bcol[h] for h in heads]
        kk = [_bdot_nt(kb[h], kh[h]) for h in heads]
        qk = [_bdot_nt(qh[h], kh[h]) for h in heads]
        st = [state_ref[h] for h in heads]
        q_st = [_bdot(qh[h] * col(eg, h), st[h]) for h in heads]
        decay = []
        for h in heads:
            diff = col(gc, h) - gct[nh + h:nh + h + 1, :]
            decay.append(jnp.where(incl, jnp.exp(jnp.where(incl, diff, 0.0)), 0.0))
        t_mat = _unit_lower_inverse([jnp.where(strict, kk[h] * decay[h], 0.0) for h in heads])
        uw = [_bdot(t_mat[h], jnp.concatenate([v_s[rs, hs[h]] * bcol[h], kb[h] * col(eg, h)], axis=1))
              for h in heads]
        w_st = [_bdot(uw[h][:, hd:], st[h]) for h in heads]
        v_new = [uw[h][:, :hd] - w_st[h] for h in heads]
        o = [q_st[h] + _bdot(qk[h] * decay[h], v_new[h]) for h in heads]
        upd = [_bdot((kh[h] * col(e_rem, h)).T, v_new[h]) for h in heads]
        for h in heads:
            state_ref[h] = st[h] * col(e_last, h) + upd[h]
            on = o[h] * lax.rsqrt(jnp.mean(o[h] * o[h], -1, keepdims=True) + NORM_EPS) * nw
            o_ref[rs, hs[h]] = (on * _silu(z_ref[rs, hs[h]])).astype(o_ref.dtype)
        return carry

    lax.fori_loop(0, rows // CHUNK, chunk_body, 0)


def gdn_mixer(proj, conv_w, alog_row, dtb_row, norm_w, *, batch, rows=256):
    t = proj.shape[0]
    s = t // batch
    rows = min(rows, s)
    nr = s // rows
    nqkv = 3 * B_HEADS * B_HEAD_DIM
    nz = B_HEADS * B_HEAD_DIM
    const = lambda shape: pl.BlockSpec(shape, lambda b, r: (0, 0))
    return pl.pallas_call(
        _gdn_kernel,
        grid=(batch, nr),
        in_specs=[pl.BlockSpec((rows, nqkv), lambda b, r: (b * nr + r, 0)),
                  pl.BlockSpec((rows, nz), lambda b, r: (b * nr + r, nqkv // nz)),
                  pl.BlockSpec((rows, LANE), lambda b, r: (b * nr + r, (nqkv + nz) // LANE)),
                  const((CONV_WIDTH, nqkv)), const((1, LANE)), const((1, LANE)), const((1, LANE))],
        out_specs=pl.BlockSpec((rows, nz), lambda b, r: (b * nr + r, 0)),
        out_shape=jax.ShapeDtypeStruct((t, nz), BF16),
        scratch_shapes=[pltpu.VMEM((8, nqkv), F32), pltpu.VMEM((rows + 8, nqkv), F32),
                        pltpu.VMEM((rows, nz), F32), pltpu.VMEM((rows, nz), F32), pltpu.VMEM((rows, nz), F32),
                        pltpu.VMEM((rows, LANE), F32), pltpu.VMEM((rows, LANE), F32),
                        pltpu.VMEM((B_HEADS, B_HEAD_DIM, B_HEAD_DIM), F32)],
        compiler_params=_cparams("parallel", "arbitrary"),
        name="gdn",
    )(proj, proj, proj, conv_w, alog_row, dtb_row, norm_w)


def _mla_prep_kernel(q_ref, kr_ref, ci_ref, slo_ref, shi_ref, qo_ref, kro_ref):
    ci, slo, shi = ci_ref[...], slo_ref[...], shi_ref[...]
    scale = (C_NOPE + C_ROPE) ** -0.5 * math.log2(math.e)
    for h in range(C_HEADS):
        base = h * 2 * LANE
        qo_ref[:, base:base + LANE] = (q_ref[:, base:base + LANE] * scale).astype(BF16)
        qo_ref[:, base + LANE:base + 2 * LANE] = (
            _rope64(q_ref[:, base + LANE:base + 2 * LANE], ci, slo, shi) * scale).astype(BF16)
    kro_ref[...] = _rope64(kr_ref[...], ci, slo, shi).astype(BF16)


def mla_prep(q_raw, proj_c, tabs, *, tm=512):
    t, n = q_raw.shape
    tm = min(tm, t)
    row = lambda w: pl.BlockSpec((tm, w), lambda i: (i, 0))
    return pl.pallas_call(
        _mla_prep_kernel,
        grid=(t // tm,),
        in_specs=[row(n), pl.BlockSpec((tm, LANE), lambda i: (i, (C_Q_RANK + C_KV_RANK) // LANE)),
                  row(LANE), row(LANE), row(LANE)],
        out_specs=[row(n), row(LANE)],
        out_shape=[jax.ShapeDtypeStruct((t, n), BF16), jax.ShapeDtypeStruct((t, LANE), BF16)],
        compiler_params=_cparams("parallel"),
        name="mla_prep",
    )(q_raw, proj_c, *tabs)


def _mla_kernel(q_ref, kv_ref, kr_ref, o_ref, m_ref, l_ref, acc_ref, *, hb):
    qi = pl.program_id(2)
    tq = q_ref.shape[0]
    heads = range(hb)
    row = lax.broadcasted_iota(I32, (tq, tq), 0)
    col = lax.broadcasted_iota(I32, (tq, tq), 1)
    diag_mask = (col // CHUNK) <= (row // CHUNK)
    m_ref[...] = jnp.full(m_ref.shape, NEG_INF, F32)
    l_ref[...] = jnp.zeros(l_ref.shape, F32)
    acc_ref[...] = jnp.zeros(acc_ref.shape, F32)
    qs = [q_ref[:, h * 2 * LANE:(h + 1) * 2 * LANE] for h in heads]

    def step(off, on_diagonal):
        rows = pl.ds(off, tq)
        kr = kr_ref[rows, :]
        ss = [_dot_nt(qs[h], jnp.concatenate([kv_ref[rows, h * 2 * LANE:h * 2 * LANE + LANE], kr], axis=1))
              for h in heads]
        if on_diagonal:
            ss = [jnp.where(diag_mask, s, NEG_INF) for s in ss]
        m_old = [m_ref[h] for h in heads]
        m_new = [jnp.maximum(m_old[h], jnp.max(ss[h], axis=1, keepdims=True)) for h in heads]
        ps = [jnp.exp2(ss[h] - m_new[h]) for h in heads]
        pv = [_dot(ps[h].astype(BF16), kv_ref[rows, h * 2 * LANE + LANE:(h + 1) * 2 * LANE]) for h in heads]
        for h in heads:
            alpha = jnp.exp2(m_old[h] - m_new[h])
            l_ref[h] = alpha * l_ref[h] + jnp.sum(ps[h], axis=1, keepdims=True)
            acc_ref[h] = alpha * acc_ref[h] + pv[h]
            m_ref[h] = m_new[h]

    def body(t, carry):
        step(pl.multiple_of(t * tq, tq), False)
        return carry

    lax.fori_loop(0, qi, body, 0)
    step(pl.multiple_of(qi * tq, tq), True)
    for h in heads:
        o_ref[:, h * LANE:(h + 1) * LANE] = (acc_ref[h] / l_ref[h]).astype(o_ref.dtype)


def mla_attention(q, kv, kr, *, tq=512, hb=4):
    b, s, _ = q.shape
    tq = min(tq, s)
    return pl.pallas_call(
        functools.partial(_mla_kernel, hb=hb),
        grid=(b, C_HEADS // hb, s // tq),
        in_specs=[pl.BlockSpec((None, tq, hb * 2 * LANE), lambda bi, h, qi: (bi, qi, h)),
                  pl.BlockSpec((None, s, hb * 2 * LANE), lambda bi, h, qi: (bi, 0, h)),
                  pl.BlockSpec((None, s, LANE), lambda bi, h, qi: (bi, 0, 0))],
        out_specs=pl.BlockSpec((None, tq, hb * LANE), lambda bi, h, qi: (bi, qi, h)),
        out_shape=jax.ShapeDtypeStruct((b, s, C_HEADS * C_V), BF16),
        scratch_shapes=[pltpu.VMEM((hb, tq, 1), F32), pltpu.VMEM((hb, tq, 1), F32),
                        pltpu.VMEM((hb, tq, LANE), F32)],
        compiler_params=_cparams("parallel", "parallel", "arbitrary"),
        name="mla_attention",
    )(q, kv, kr)


def _ssd_kernel(zx_ref, misc_ref, convw_ref, convb_ref, dtb_ref, a_ref, aexp_ref, dskip_ref,
                nw_ref, expand_ref, o_ref,
                tail_ref, xp_ref, xs_s, xdt_s, bm_s, cm_s, la_s, laexp_s, y_s, state_ref):
    rows = zx_ref.shape[0]
    pd = D_HEAD_DIM
    gw = D_INNER // D_GROUPS
    pairs_per_group = gw // LANE

    @pl.when(pl.program_id(1) == 0)
    def _():
        tail_ref[...] = jnp.zeros(tail_ref.shape, F32)
        state_ref[...] = jnp.zeros(state_ref.shape, F32)

    y = _silu(_causal_conv(zx_ref[:, D_INNER:], tail_ref, xp_ref, convw_ref[...]) + convb_ref[...])
    xs = y[:, :D_INNER]
    xs_s[...] = xs
    bm_s[...] = y[:, D_INNER:D_INNER + D_GROUPS * D_STATE]
    cm_s[...] = y[:, D_INNER + D_GROUPS * D_STATE:]
    dt = _softplus(misc_ref[...] + dtb_ref[...])
    dt_exp = _dot01_right(dt, expand_ref[...])
    xdt_s[...] = xs * dt_exp
    la_s[...] = dt * a_ref[...]
    laexp_s[...] = dt_exp * aexp_ref[...]

    ri = lax.broadcasted_iota(I32, (CHUNK, CHUNK), 0)
    ci = lax.broadcasted_iota(I32, (CHUNK, CHUNK), 1)
    incl = ri >= ci
    lower01 = incl.astype(BF16)
    lane_lo = lax.broadcasted_iota(I32, (CHUNK, LANE), 1) < pd

    def chunk_body(c, carry):
        r0 = pl.multiple_of(c * CHUNK, CHUNK)
        rs = pl.ds(r0, CHUNK)
        cum = _dot01_left(lower01, la_s[rs, :])
        cumt = cum.T
        cum_e = _dot01_left(lower01, laexp_s[rs, :])
        ecum = jnp.exp(cum_e)
        last = cum_e[CHUNK - 1:CHUNK, :]
        dec_st = jnp.exp(last - cum_e)
        cdec = jnp.exp(last)
        for g in range(D_GROUPS):
            bm = bm_s[rs, g * D_STATE:(g + 1) * D_STATE]
            cm = cm_s[rs, g * D_STATE:(g + 1) * D_STATE]
            cb = _bdot_nt(cm, bm)
            bmt = bm.T.astype(BF16)
            gs = slice(g * gw, (g + 1) * gw)
            y_off = _bdot(cm, state_ref[:, gs]) * ecum[:, gs]
            for pp in range(pairs_per_group):
                pidx = g * pairs_per_group + pp
                ps = slice(pidx * LANE, (pidx + 1) * LANE)
                xp = xdt_s[rs, ps]
                ys = []
                for hh in range(2):
                    h = 2 * pidx + hh
                    diff = cum[:, h:h + 1] - cumt[h:h + 1, :]
                    lmat = jnp.where(incl, jnp.exp(jnp.where(incl, diff, 0.0)), 0.0)
                    ys.append(_bdot(cb * lmat, xp))
                y_diag = jnp.where(lane_lo, ys[0], ys[1])
                y_s[rs, ps] = (y_diag + y_off[:, pp * LANE:(pp + 1) * LANE]
                               + dskip_ref[:, ps] * xs_s[rs, ps])
                s_new = _dot(bmt, (xp * dec_st[:, ps]).astype(BF16))
                state_ref[:, ps] = state_ref[:, ps] * cdec[:, ps] + s_new
        return carry

    lax.fori_loop(0, rows // CHUNK, chunk_body, 0)
    yv = y_s[...] * _silu(zx_ref[:, :D_INNER])
    for g in range(D_GROUPS):
        gs = slice(g * gw, (g + 1) * gw)
        yg = yv[:, gs]
        ms = jnp.mean(yg * yg, axis=-1, keepdims=True)
        o_ref[:, gs] = (yg * lax.rsqrt(ms + NORM_EPS) * nw_ref[:, gs]).astype(o_ref.dtype)


def ssd_mixer(proj, conv_w, conv_b, dtb_row, a_row, a_exp, dskip_exp, norm_w, expand, *, batch, rows=256):
    t = proj.shape[0]
    s = t // batch
    rows = min(rows, s)
    nr = s // rows
    nx = D_INNER + 2 * D_GROUPS * D_STATE
    const = lambda shape: pl.BlockSpec(shape, lambda b, r: (0, 0))
    return pl.pallas_call(
        _ssd_kernel,
        grid=(batch, nr),
        in_specs=[pl.BlockSpec((rows, D_INNER + nx), lambda b, r: (b * nr + r, 0)),
                  pl.BlockSpec((rows, LANE), lambda b, r: (b * nr + r, (D_INNER + nx) // LANE)),
                  const((CONV_WIDTH, nx)), const((1, nx)), const((1, LANE)), const((1, LANE)),
                  const((1, D_INNER)), const((1, D_INNER)), const((1, D_INNER)), const((LANE, D_INNER))],
        out_specs=pl.BlockSpec((rows, D_INNER), lambda b, r: (b * nr + r, 0)),
        out_shape=jax.ShapeDtypeStruct((t, D_INNER), BF16),
        scratch_shapes=[pltpu.VMEM((8, nx), F32), pltpu.VMEM((rows + 8, nx), F32),
                        pltpu.VMEM((rows, D_INNER), F32), pltpu.VMEM((rows, D_INNER), F32),
                        pltpu.VMEM((rows, D_GROUPS * D_STATE), F32), pltpu.VMEM((rows, D_GROUPS * D_STATE), F32),
                        pltpu.VMEM((rows, LANE), F32), pltpu.VMEM((rows, D_INNER), F32),
                        pltpu.VMEM((rows, D_INNER), F32),
                        pltpu.VMEM((D_STATE, D_INNER), F32)],
        compiler_params=_cparams("parallel", "arbitrary"),
        name="ssd",
    )(proj, proj, conv_w, conv_b, dtb_row, a_row, a_exp, dskip_exp, norm_w, expand)


def _rope_tables(positions):
    pos = positions.reshape(-1).astype(F32)[:, None]

    def cs(dim):
        inv = jnp.power(ROPE_THETA, -jnp.arange(0, dim, 2, dtype=F32) / dim)
        ang = pos * inv
        return jnp.cos(ang), jnp.sin(ang)

    c, s = cs(A_HEAD_DIM)
    tab_a = (jnp.concatenate([c, c], -1), jnp.concatenate([-s, s], -1))
    c, s = cs(IDX_DIM)
    z = jnp.zeros_like(s)
    tab_i = (jnp.concatenate([c, c, c, c], -1), jnp.concatenate([-s, z, -s, z], -1),
             jnp.concatenate([z, s, z, s], -1))
    return tab_a, tab_i


def _pad_cols(w, n):
    return jnp.pad(w, ((0, 0), (0, n - w.shape[1])))


def _lane_row(v, start):
    return jnp.zeros((1, LANE), F32).at[0, start:start + v.shape[0]].set(v.astype(F32))


def _moe_weights(i, w_group, b_group, w_expert, b_expert, w_gate, w_up, w_down):
    w_router = jnp.concatenate([_pad_cols(w_group[i], LANE), _pad_cols(w_expert[i], LANE)], axis=1).astype(F32)
    b_router = jnp.concatenate([_pad_cols(b_group[i][None], LANE), _pad_cols(b_expert[i][None], LANE)],
                               axis=1).astype(F32)
    w_gu = jnp.concatenate([w_gate[i], w_up[i]], axis=-1).astype(BF16)
    return w_router, b_router, w_gu, w_down[i].astype(BF16)


def kernel(x, p, positions, norm_mix, norm_ffn, norm_ple, norm_final, ev_w_in, ev_w_out, gdn_conv_w,
           gdn_a_log, gdn_dt_bias, gdn_norm, od_w_in, od_w_out, mla_q_norm, mla_kv_norm, mla_w_uq,
           mla_w_ukv, ssm_conv_w, ssm_conv_b, ssm_a_log, ssm_dt_bias, ssm_d_skip, ssm_norm, moe_w_group,
           moe_b_group, moe_w_expert, moe_b_expert, moe_w_gate, moe_w_up, moe_w_down, ple_w_proj,
           ple_w_gate):
    b, s, d = x.shape
    t = b * s
    tab_a, tab_i = _rope_tables(positions)
    h = x.reshape(t, d)
    moe_args = (moe_w_group, moe_b_group, moe_w_expert, moe_b_expert, moe_w_gate, moe_w_up, moe_w_down)

    w = ev_w_in[0]
    o = np_cumsum((1024, 256, 256, 512, 64, 8, 3072, 1024, 8, 8))
    w_a = _pad_cols(w[:, :o[6]], 2176).astype(BF16)
    w_b = _pad_cols(jnp.concatenate([w[:, o[6]:o[8]], w[:, o[8]:]], axis=1), 4224).astype(BF16)
    proj_a = norm_matmul(h, norm_mix[0], w_a, k=d)
    proj_b = norm_matmul(h, norm_mix[0], w_b, k=d, tn=1408)
    q, k, v, iq, ik, iw = dsa_prep(proj_a, tab_a + tab_i)
    r3 = lambda a: a.reshape(b, s, a.shape[-1])
    o_a = dsa_attention(r3(q), r3(k), r3(v), r3(iq), r3(ik), r3(iw)).reshape(t, -1)
    o_b = gdn_mixer(proj_b, gdn_conv_w[0].astype(F32), _lane_row(gdn_a_log[0], B_HEADS),
                    _lane_row(gdn_dt_bias[0], B_HEADS), gdn_norm[0].reshape(1, -1).astype(F32), batch=b)
    h = outproj_residual(o_a, o_b, ev_w_out[0].astype(BF16), h)
    h = moe_residual(h, norm_ffn[0], *_moe_weights(0, *moe_args))
    h = ple_residual(h, norm_ple[0], ple_w_gate[0].astype(BF16), p[0].reshape(t, -1),
                     ple_w_proj[0].astype(BF16))

    w = od_w_in[0]
    o = np_cumsum((512, 512, 64, 1024, 1536, 16))
    w_c = _pad_cols(w[:, :o[3]], 1152).astype(BF16)
    w_d = _pad_cols(w[:, o[3]:], 2688).astype(BF16)
    proj_c = norm_matmul(h, norm_mix[1], w_c, k=d)
    proj_d = norm_matmul(h, norm_mix[1], w_d, k=d, tn=896)
    w_uq = jnp.pad(mla_w_uq[0].reshape(C_Q_RANK, C_HEADS, C_NOPE + C_ROPE),
                   ((0, 0), (0, 0), (0, 2 * LANE - C_NOPE - C_ROPE))).reshape(C_Q_RANK, -1).astype(BF16)
    q_raw = norm_matmul(proj_c, mla_q_norm[0], w_uq, k=C_Q_RANK, xblk=0, tn=1024)
    kv = norm_matmul(proj_c, mla_kv_norm[0], mla_w_ukv[0].astype(BF16), k=C_KV_RANK, xblk=1, tn=1024,
                     out_dtype=BF16)
    q, kr = mla_prep(q_raw, proj_c, tab_i)
    o_c = mla_attention(r3(q), r3(kv), r3(kr)).reshape(t, -1)
    a_heads = -jnp.exp(ssm_a_log[0].astype(F32))
    expand = (jnp.arange(LANE)[:, None] == (jnp.arange(D_INNER)[None, :] // D_HEAD_DIM)).astype(BF16)
    o_d = ssd_mixer(proj_d, ssm_conv_w[0].astype(F32), ssm_conv_b[0].reshape(1, -1).astype(F32),
                    _lane_row(ssm_dt_bias[0], 0), _lane_row(a_heads, 0),
                    jnp.repeat(a_heads, D_HEAD_DIM)[None], jnp.repeat(ssm_d_skip[0].astype(F32), D_HEAD_DIM)[None],
                    ssm_norm[0].reshape(1, -1).astype(F32), expand, batch=b)
    h = outproj_residual(o_c, o_d, od_w_out[0].astype(BF16), h)
    h = moe_residual(h, norm_ffn[1], *_moe_weights(1, *moe_args))
    h = ple_residual(h, norm_ple[1], ple_w_gate[1].astype(BF16), p[1].reshape(t, -1),
                     ple_w_proj[1].astype(BF16))
    return rmsnorm(h, norm_final).reshape(b, s, d)


def np_cumsum(sizes):
    out, acc = [], 0
    for v in sizes:
        out.append(acc)
        acc += v
    return out
```

```python
import functools
import math

import jax
import jax.numpy as jnp
from jax import lax
from jax.experimental import pallas as pl
from jax.experimental.pallas import tpu as pltpu

F32 = jnp.float32
BF16 = jnp.bfloat16
I32 = jnp.int32

CHUNK = 64
ROPE_THETA = 10000.0
NORM_EPS = 1e-6
NEG_INF = -1e30
BISECT_STEPS = 24

A_HEADS, A_KV_HEADS, A_HEAD_DIM = 8, 2, 128
IDX_HEADS, IDX_DIM = 8, 64
IDX_SCALE = (IDX_HEADS * IDX_DIM) ** -0.5
TOPK_MAX = 256
B_HEADS, B_HEAD_DIM, CONV_WIDTH = 8, 128, 4
C_HEADS, C_Q_RANK, C_KV_RANK, C_NOPE, C_ROPE, C_V = 8, 512, 512, 128, 64, 128
D_INNER, D_HEAD_DIM, D_GROUPS, D_STATE = 1024, 64, 2, 128
D_HEADS = D_INNER // D_HEAD_DIM
N_GROUPS, EXPERTS_PER_GROUP, EXPERT_FF = 4, 8, 256
N_EXPERTS = N_GROUPS * EXPERTS_PER_GROUP

LANE = 128
VMEM_BYTES_V7X = 64 * 1024 * 1024
VMEM_LIMIT = VMEM_BYTES_V7X - 8 * 1024 * 1024


def _cparams(*sem):
    return pltpu.CompilerParams(dimension_semantics=sem, vmem_limit_bytes=VMEM_LIMIT)


def _dot(a, b):
    return jnp.dot(a, b, preferred_element_type=F32)


def _dot_nt(a, b):
    return lax.dot_general(a, b, (((1,), (1,)), ((), ())), preferred_element_type=F32)


def _bdot(a, b):
    return _dot(a.astype(BF16), b.astype(BF16))


def _bdot_nt(a, b):
    return _dot_nt(a.astype(BF16), b.astype(BF16))


def _split3(x):
    hi = x.astype(BF16)
    r1 = x - hi.astype(F32)
    mid = r1.astype(BF16)
    lo = (r1 - mid.astype(F32)).astype(BF16)
    return hi, mid, lo


def _dot01_left(m01, x):
    hi, mid, lo = _split3(x)
    return _dot(m01, hi) + _dot(m01, mid) + _dot(m01, lo)


def _dot01_right(x, m01):
    hi, mid, lo = _split3(x)
    return _dot(hi, m01) + _dot(mid, m01) + _dot(lo, m01)


def _sigmoid(x):
    return 1.0 / (1.0 + jnp.exp(-x))


def _silu(x):
    return x * _sigmoid(x)


def _softplus(x):
    return jnp.maximum(x, 0.0) + jnp.log1p(jnp.exp(-jnp.abs(x)))


def _lane_fold(x, op):
    blocks = [x[:, c * LANE:(c + 1) * LANE] for c in range(x.shape[1] // LANE)]
    while len(blocks) > 1:
        blocks = [op(blocks[i], blocks[i + 1]) for i in range(0, len(blocks), 2)]
    return blocks[0]


def _rowmax(x):
    return jnp.max(_lane_fold(x, jnp.maximum), axis=1, keepdims=True)


def _rowsum(x):
    return jnp.sum(_lane_fold(x, jnp.add), axis=1, keepdims=True)


def _rope64(x, c, slo, shi):
    return x * c + pltpu.roll(x, 96, 1) * slo + pltpu.roll(x, 32, 1) * shi


def _rope128(x, c, s):
    return x * c + pltpu.roll(x, 64, 1) * s


def _norm_matmul_kernel(x_ref, g_ref, w_ref, o_ref, xn_ref):
    @pl.when(pl.program_id(1) == 0)
    def _():
        x = x_ref[...].astype(F32)
        ms = jnp.mean(x * x, axis=-1, keepdims=True)
        xn_ref[...] = (x * lax.rsqrt(ms + NORM_EPS) * g_ref[...]).astype(BF16)

    o_ref[...] = _dot(xn_ref[...], w_ref[...]).astype(o_ref.dtype)


def norm_matmul(x, g, w, *, k, xblk=0, tm=512, tn=None, out_dtype=F32):
    t = x.shape[0]
    n = w.shape[1]
    tn = n if tn is None else tn
    tm = min(tm, t)
    return pl.pallas_call(
        _norm_matmul_kernel,
        grid=(t // tm, n // tn),
        in_specs=[pl.BlockSpec((tm, k), lambda i, j: (i, xblk)),
                  pl.BlockSpec((1, k), lambda i, j: (0, 0)),
                  pl.BlockSpec((k, tn), lambda i, j: (0, j))],
        out_specs=pl.BlockSpec((tm, tn), lambda i, j: (i, j)),
        out_shape=jax.ShapeDtypeStruct((t, n), out_dtype),
        scratch_shapes=[pltpu.VMEM((tm, k), BF16)],
        compiler_params=_cparams("parallel", "arbitrary"),
        name="norm_matmul",
    )(x, g.reshape(1, k).astype(F32), w)


def _outproj_kernel(a1_ref, a2_ref, w1_ref, w2_ref, r_ref, o_ref):
    o_ref[...] = r_ref[...] + _dot(a1_ref[...], w1_ref[...]) + _dot(a2_ref[...], w2_ref[...])


def outproj_residual(a1, a2, w, res, *, tm=512, tn=1024):
    t, kh = a1.shape
    n = w.shape[1]
    tm = min(tm, t)
    return pl.pallas_call(
        _outproj_kernel,
        grid=(t // tm, n // tn),
        in_specs=[pl.BlockSpec((tm, kh), lambda i, j: (i, 0)),
                  pl.BlockSpec((tm, kh), lambda i, j: (i, 0)),
                  pl.BlockSpec((kh, tn), lambda i, j: (0, j)),
                  pl.BlockSpec((kh, tn), lambda i, j: (1, j)),
                  pl.BlockSpec((tm, tn), lambda i, j: (i, j))],
        out_specs=pl.BlockSpec((tm, tn), lambda i, j: (i, j)),
        out_shape=jax.ShapeDtypeStruct((t, n), F32),
        compiler_params=_cparams("parallel", "arbitrary"),
        name="outproj",
    )(a1, a2, w, w, res)


def _ple_kernel(h_ref, g_ref, wg_ref, p_ref, wp_ref, hres_ref, o_ref, xn_ref):
    @pl.when(pl.program_id(1) == 0)
    def _():
        x = h_ref[...]
        ms = jnp.mean(x * x, axis=-1, keepdims=True)
        xn_ref[...] = (x * lax.rsqrt(ms + NORM_EPS) * g_ref[...]).astype(BF16)

    gate = _sigmoid(_dot(xn_ref[...], wg_ref[...]))
    emb = _dot(p_ref[...].astype(BF16), wp_ref[...])
    o_ref[...] = hres_ref[...] + gate * emb


def ple_residual(h, g, wg, p, wp, *, tm=512, tn=1024):
    t, d = h.shape
    pd = p.shape[1]
    tm = min(tm, t)
    return pl.pallas_call(
        _ple_kernel,
        grid=(t // tm, d // tn),
        in_specs=[pl.BlockSpec((tm, d), lambda i, j: (i, 0)),
                  pl.BlockSpec((1, d), lambda i, j: (0, 0)),
                  pl.BlockSpec((d, tn), lambda i, j: (0, j)),
                  pl.BlockSpec((tm, pd), lambda i, j: (i, 0)),
                  pl.BlockSpec((pd, tn), lambda i, j: (0, j)),
                  pl.BlockSpec((tm, tn), lambda i, j: (i, j))],
        out_specs=pl.BlockSpec((tm, tn), lambda i, j: (i, j)),
        out_shape=jax.ShapeDtypeStruct((t, d), F32),
        scratch_shapes=[pltpu.VMEM((tm, d), BF16)],
        compiler_params=_cparams("parallel", "arbitrary"),
        name="ple",
    )(h, g.reshape(1, d).astype(F32), wg, p, wp, h)


def _rmsnorm_kernel(x_ref, g_ref, o_ref):
    x = x_ref[...]
    ms = jnp.mean(x * x, axis=-1, keepdims=True)
    o_ref[...] = x * lax.rsqrt(ms + NORM_EPS) * g_ref[...]


def rmsnorm(x, g, *, tm=512):
    t, d = x.shape
    tm = min(tm, t)
    return pl.pallas_call(
        _rmsnorm_kernel,
        grid=(t // tm,),
        in_specs=[pl.BlockSpec((tm, d), lambda i: (i, 0)),
                  pl.BlockSpec((1, d), lambda i: (0, 0))],
        out_specs=pl.BlockSpec((tm, d), lambda i: (i, 0)),
        out_shape=jax.ShapeDtypeStruct((t, d), F32),
        compiler_params=_cparams("parallel"),
        name="final_norm",
    )(x, g.reshape(1, d).astype(F32))


def _rms_normed(h_ref, g_ref):
    x = h_ref[...]
    ms = jnp.mean(x * x, axis=-1, keepdims=True)
    return x * lax.rsqrt(ms + NORM_EPS) * g_ref[...]


def _route(xn, wr, br):
    n = xn.shape[0]
    logits = jnp.dot(xn, wr, preferred_element_type=F32, precision=lax.Precision.HIGHEST) + br
    lane = lax.broadcasted_iota(I32, (n, LANE), 1)
    glog = jnp.where(lane < N_GROUPS, logits, NEG_INF)
    gmax = jnp.max(glog, axis=-1, keepdims=True)
    g_sel = jnp.min(jnp.where(glog == gmax, lane, LANE), axis=-1, keepdims=True)
    p_sel = 1.0 / jnp.sum(jnp.exp(glog - gmax), axis=-1, keepdims=True)
    eidx = lane - N_GROUPS
    in_grp = (eidx >= 0) & (eidx < N_EXPERTS) & ((eidx // EXPERTS_PER_GROUP) == g_sel)
    e1 = jnp.where(in_grp, logits, NEG_INF)
    v1 = jnp.max(e1, axis=-1, keepdims=True)
    i1 = jnp.min(jnp.where(e1 == v1, eidx, LANE), axis=-1, keepdims=True)
    e2 = jnp.where(eidx == i1, NEG_INF, e1)
    v2 = jnp.max(e2, axis=-1, keepdims=True)
    i2 = jnp.min(jnp.where(e2 == v2, eidx, LANE), axis=-1, keepdims=True)
    ex = jnp.exp(v2 - v1)
    return i1, i2, p_sel / (1.0 + ex), p_sel * ex / (1.0 + ex)


def _moe_kernel(h_ref, g_ref, wr_ref, br_ref, wgu_ref, wd_ref, o_ref, xn_ref, comb_ref, *, ke):
    e = pl.program_id(1)
    tm = h_ref.shape[0]

    @pl.when(e == 0)
    def _():
        xn = _rms_normed(h_ref, g_ref)
        xn_ref[...] = xn.astype(BF16)
        o_ref[...] = h_ref[...]
        i1, i2, w1, w2 = _route(xn, wr_ref[...], br_ref[...])
        lane = lax.broadcasted_iota(I32, (tm, LANE), 1)
        comb_ref[...] = jnp.where(lane == i1, w1, 0.0) + jnp.where(lane == i2, w2, 0.0)

    xn = xn_ref[...]
    lane = lax.broadcasted_iota(I32, (tm, LANE), 1)
    comb = comb_ref[...]
    hids = []
    for j in range(ke):
        gu = _dot(xn, wgu_ref[j])
        c = jnp.sum(jnp.where(lane == e * ke + j, comb, 0.0), axis=-1, keepdims=True)
        hid = _silu(gu[:, :EXPERT_FF]) * gu[:, EXPERT_FF:] * c
        hids.append(hid.astype(BF16))
    hid = jnp.concatenate(hids, axis=1)
    wd = wd_ref[...].reshape(ke * EXPERT_FF, wd_ref.shape[2])
    o_ref[...] += _dot(hid, wd)


def moe_residual(h, g, w_router, b_router, w_gu, w_down, *, tm=512, ke=4):
    t, d = h.shape
    tm = min(tm, t)
    return pl.pallas_call(
        functools.partial(_moe_kernel, ke=ke),
        grid=(t // tm, N_EXPERTS // ke),
        in_specs=[pl.BlockSpec((tm, d), lambda i, e: (i, 0)),
                  pl.BlockSpec((1, d), lambda i, e: (0, 0)),
                  pl.BlockSpec((d, LANE), lambda i, e: (0, 0)),
                  pl.BlockSpec((1, LANE), lambda i, e: (0, 0)),
                  pl.BlockSpec((ke, d, 2 * EXPERT_FF), lambda i, e: (e, 0, 0)),
                  pl.BlockSpec((ke, EXPERT_FF, d), lambda i, e: (e, 0, 0))],
        out_specs=pl.BlockSpec((tm, d), lambda i, e: (i, 0)),
        out_shape=jax.ShapeDtypeStruct((t, d), F32),
        scratch_shapes=[pltpu.VMEM((tm, d), BF16), pltpu.VMEM((tm, LANE), F32)],
        compiler_params=_cparams("parallel", "arbitrary"),
        name="moe",
    )(h, g.reshape(1, d).astype(F32), w_router, b_router, w_gu, w_down)


MOE_SUB = 256
MOE_CAP = 32
MOE_SLOTS = N_EXPERTS * MOE_CAP


def _moe_gather_kernel(h_ref, g_ref, wr_ref, br_ref, xc_ref, meta_ref, flag_ref):
    n = h_ref.shape[0]
    xn = _rms_normed(h_ref, g_ref)
    i1, i2, w1, w2 = _route(xn, wr_ref[...], br_ref[...])
    lane = lax.broadcasted_iota(I32, (n, LANE), 1)
    onehot = jnp.where((lane == i1) | (lane == i2), 1.0, 0.0)
    earlier = (lax.broadcasted_iota(I32, (n, n), 1) < lax.broadcasted_iota(I32, (n, n), 0)).astype(BF16)
    rank = _dot(earlier, onehot.astype(BF16))
    r1 = jnp.sum(jnp.where(lane == i1, rank, 0.0), axis=-1, keepdims=True)
    r2 = jnp.sum(jnp.where(lane == i2, rank, 0.0), axis=-1, keepdims=True)
    fits = (r1 < MOE_CAP) & (r2 < MOE_CAP)
    flag_ref[...] = jnp.broadcast_to(jnp.max(jnp.where(fits, 0.0, 1.0), axis=0, keepdims=True), flag_ref.shape)
    pos1 = jnp.where(r1 < MOE_CAP, i1.astype(F32) * MOE_CAP + r1, -1.0)
    pos2 = jnp.where(r2 < MOE_CAP, i2.astype(F32) * MOE_CAP + r2, -1.0)
    meta = (jnp.where(lane == 0, pos1, 0.0) + jnp.where(lane == 1, pos2, 0.0)
            + jnp.where(lane == 2, w1, 0.0) + jnp.where(lane == 3, w2, 0.0))
    meta_ref[...] = meta
    meta_t = meta.T
    slot = lax.broadcasted_iota(I32, (MOE_SLOTS, n), 0).astype(F32)
    pick = jnp.where((slot == meta_t[0:1, :]) | (slot == meta_t[1:2, :]), 1.0, 0.0).astype(BF16)
    xc = _dot(pick, xn.astype(BF16)).astype(BF16)
    xc_ref[...] = xc.reshape(xc_ref.shape)


def _moe_ffn_kernel(x_ref, wg_ref, wu_ref, wd_ref, y_ref):
    x = x_ref[...]
    hid = _silu(_dot(x, wg_ref[...].astype(BF16))) * _dot(x, wu_ref[...].astype(BF16))
    y_ref[...] = _dot(hid.astype(BF16), wd_ref[...].astype(BF16)).astype(y_ref.dtype)


def _moe_scatter_kernel(y_ref, meta_ref, h_ref, o_ref):
    n = h_ref.shape[0]
    meta = meta_ref[...]
    slot = lax.broadcasted_iota(I32, (n, MOE_SLOTS), 1).astype(F32)
    y = y_ref[...].reshape(MOE_SLOTS, y_ref.shape[-1])
    out = h_ref[...]
    for j in range(2):
        pick = jnp.where(slot == meta[:, j:j + 1], 1.0, 0.0).astype(BF16)
        out = out + meta[:, 2 + j:3 + j] * _dot(pick, y)
    o_ref[...] = out


def moe_routed(h, g, w_router, b_router, w_gate, w_up, w_down, *, tm=1024):
    t, d = h.shape
    nsub = t // MOE_SUB
    rows = nsub * MOE_CAP
    tm = min(tm, rows)
    xc, meta, flag = pl.pallas_call(
        _moe_gather_kernel,
        grid=(nsub,),
        in_specs=[pl.BlockSpec((MOE_SUB, d), lambda i: (i, 0)),
                  pl.BlockSpec((1, d), lambda i: (0, 0)),
                  pl.BlockSpec((d, LANE), lambda i: (0, 0)),
                  pl.BlockSpec((1, LANE), lambda i: (0, 0))],
        out_specs=[pl.BlockSpec((N_EXPERTS, None, MOE_CAP, d), lambda i: (0, i, 0, 0)),
                   pl.BlockSpec((MOE_SUB, LANE), lambda i: (i, 0)),
                   pl.BlockSpec((None, 8, LANE), lambda i: (i, 0, 0))],
        out_shape=[jax.ShapeDtypeStruct((N_EXPERTS, nsub, MOE_CAP, d), BF16),
                   jax.ShapeDtypeStruct((t, LANE), F32),
                   jax.ShapeDtypeStruct((nsub, 8, LANE), F32)],
        compiler_params=_cparams("parallel"),
        name="moe_gather",
    )(h, g.reshape(1, d).astype(F32), w_router, b_router)
    y = pl.pallas_call(
        _moe_ffn_kernel,
        grid=(N_EXPERTS, rows // tm),
        in_specs=[pl.BlockSpec((None, tm, d), lambda e, i: (e, i, 0)),
                  pl.BlockSpec((None, d, EXPERT_FF), lambda e, i: (e, 0, 0)),
                  pl.BlockSpec((None, d, EXPERT_FF), lambda e, i: (e, 0, 0)),
                  pl.BlockSpec((None, EXPERT_FF, d), lambda e, i: (e, 0, 0))],
        out_specs=pl.BlockSpec((None, tm, d), lambda e, i: (e, i, 0)),
        out_shape=jax.ShapeDtypeStruct((N_EXPERTS, rows, d), BF16),
        compiler_params=_cparams("parallel", "arbitrary"),
        name="moe_ffn",
    )(xc.reshape(N_EXPERTS, rows, d), w_gate, w_up, w_down)
    out = pl.pallas_call(
        _moe_scatter_kernel,
        grid=(nsub,),
        in_specs=[pl.BlockSpec((N_EXPERTS, None, MOE_CAP, d), lambda i: (0, i, 0, 0)),
                  pl.BlockSpec((MOE_SUB, LANE), lambda i: (i, 0)),
                  pl.BlockSpec((MOE_SUB, d), lambda i: (i, 0))],
        out_specs=pl.BlockSpec((MOE_SUB, d), lambda i: (i, 0)),
        out_shape=jax.ShapeDtypeStruct((t, d), F32),
        compiler_params=_cparams("parallel"),
        name="moe_scatter",
    )(y.reshape(N_EXPERTS, nsub, MOE_CAP, d), meta, h)
    return out, jnp.max(flag)


def moe_layer(h, g, w_router, b_router, w_gate, w_up, w_down):
    out, overflow = moe_routed(h, g, w_router, b_router, w_gate, w_up, w_down)

    def dense():
        w_gu = jnp.concatenate([w_gate, w_up], axis=-1).astype(BF16)
        return moe_residual(h, g, w_router, b_router, w_gu, w_down.astype(BF16))

    return lax.cond(overflow > 0.0, dense, lambda: out)


def _dsa_prep_kernel(x_ref, ca_ref, sa_ref, ci_ref, slo_ref, shi_ref,
                     q_ref, k_ref, v_ref, iq_ref, ik_ref, iw_ref):
    ca, sa = ca_ref[...], sa_ref[...]
    ci, slo, shi = ci_ref[...], slo_ref[...], shi_ref[...]
    scale = A_HEAD_DIM ** -0.5 * math.log2(math.e)
    for h in range(A_HEADS):
        xh = x_ref[:, h * LANE:(h + 1) * LANE]
        q_ref[:, h * LANE:(h + 1) * LANE] = (_rope128(xh, ca, sa) * scale).astype(BF16)
    off = A_HEADS * A_HEAD_DIM
    for h in range(A_KV_HEADS):
        xh = x_ref[:, off + h * LANE:off + (h + 1) * LANE]
        k_ref[:, h * LANE:(h + 1) * LANE] = _rope128(xh, ca, sa).astype(BF16)
    off += A_KV_HEADS * A_HEAD_DIM
    v_ref[...] = x_ref[:, off:off + A_KV_HEADS * A_HEAD_DIM].astype(BF16)
    off += A_KV_HEADS * A_HEAD_DIM
    for h in range(IDX_HEADS * IDX_DIM // LANE):
        xh = x_ref[:, off + h * LANE:off + (h + 1) * LANE]
        iq_ref[:, h * LANE:(h + 1) * LANE] = _rope64(xh, ci, slo, shi).astype(BF16)
    off += IDX_HEADS * IDX_DIM
    misc = x_ref[:, off:off + LANE]
    ik_ref[...] = _rope64(misc, ci, slo, shi)[:, :IDX_DIM].astype(BF16)
    iw_ref[...] = misc[:, IDX_DIM:IDX_DIM + IDX_HEADS] * IDX_SCALE


def dsa_prep(proj, tabs, *, tm=512):
    t, n = proj.shape
    tm = min(tm, t)
    row = lambda w: pl.BlockSpec((tm, w), lambda i: (i, 0))
    return pl.pallas_call(
        _dsa_prep_kernel,
        grid=(t // tm,),
        in_specs=[row(n)] + [row(LANE)] * 5,
        out_specs=[row(1024), row(256), row(256), row(512), row(IDX_DIM), row(IDX_HEADS)],
        out_shape=[jax.ShapeDtypeStruct((t, 1024), BF16), jax.ShapeDtypeStruct((t, 256), BF16),
                   jax.ShapeDtypeStruct((t, 256), BF16), jax.ShapeDtypeStruct((t, 512), BF16),
                   jax.ShapeDtypeStruct((t, IDX_DIM), BF16), jax.ShapeDtypeStruct((t, IDX_HEADS), F32)],
        compiler_params=_cparams("parallel"),
        name="dsa_prep",
    )(proj, *tabs)


def _dsa_kernel(q_ref, k_ref, v_ref, iq_ref, ik_ref, iw_ref, o_ref,
                key_ref, m_ref, l_ref, acc_ref, *, tk, topk):
    qi = pl.program_id(1)
    nq = q_ref.shape[0]
    rep = A_HEADS // A_KV_HEADS
    n_tiles = ((qi + 1) * nq + tk - 1) // tk
    row = lax.broadcasted_iota(I32, (nq, tk), 0)
    col = lax.broadcasted_iota(I32, (nq, tk), 1)
    q_chunk = (qi * nq + row) // CHUNK

    def admissible(off):
        return ((off + col) // CHUNK) <= q_chunk

    iq = iq_ref[...]
    iw = iw_ref[...]

    def score_body(t, carry):
        off = pl.multiple_of(t * tk, tk)
        ikt = ik_ref[pl.ds(off, tk), :]
        s = jnp.zeros((nq, tk), F32)
        for h in range(IDX_HEADS):
            rel = _dot_nt(iq[:, h * IDX_DIM:(h + 1) * IDX_DIM], ikt)
            s = s + iw[:, h:h + 1] * jnp.maximum(rel, 0.0)
        key_ref[:, pl.ds(off, tk)] = jnp.where(admissible(off), s, NEG_INF)
        return carry

    lax.fori_loop(0, n_tiles, score_body, 0)

    def scan(fn, init):
        def body(t, acc):
            off = pl.multiple_of(t * tk, tk)
            blk = key_ref[:, pl.ds(off, tk)]
            for c in range(tk // LANE):
                acc = fn(blk[:, c * LANE:(c + 1) * LANE], acc)
            return acc
        return lax.fori_loop(0, n_tiles, body, init)

    def rowsum(x):
        return jnp.sum(x, axis=1, keepdims=True)

    def count_ge(v):
        return rowsum(scan(lambda blk, acc: acc + jnp.where(blk >= v, 1.0, 0.0), jnp.zeros((nq, LANE), F32)))

    big = jnp.full((nq, LANE), -NEG_INF, F32)
    zeros = jnp.zeros((nq, LANE), F32)
    lo_p, hi_p, n_p = scan(
        lambda blk, acc: (jnp.minimum(acc[0], jnp.where(blk > NEG_INF, blk, big)), jnp.maximum(acc[1], blk),
                          acc[2] + jnp.where(blk > NEG_INF, 1.0, 0.0)), (big, -big, zeros))
    enough = rowsum(n_p) >= topk
    lo0 = jnp.where(enough, jnp.min(lo_p, axis=1, keepdims=True), NEG_INF)
    hi0 = jnp.where(enough, jnp.max(hi_p, axis=1, keepdims=True), NEG_INF)

    def bisect_body(i, bracket):
        lo, hi = bracket
        mid = lo + 0.5 * (hi - lo)
        ge = count_ge(mid) >= topk
        return jnp.where(ge, mid, lo), jnp.where(ge, hi, mid)

    _, hi = lax.fori_loop(0, BISECT_STEPS, bisect_body, (lo0, hi0))

    def walk_cond(state):
        return state[1] > 0

    def walk_body(state):
        v, _ = state
        zero = jnp.zeros((nq, LANE), F32)
        cnt_p, nxt_p = scan(lambda blk, acc: (acc[0] + jnp.where(blk >= v, 1.0, 0.0),
                                              jnp.maximum(acc[1], jnp.where(blk < v, blk, -big))),
                            (zero, -big))
        found = rowsum(cnt_p) >= topk
        v = jnp.where(found, v, jnp.max(nxt_p, axis=1, keepdims=True))
        return v, jnp.sum(jnp.where(found, 0, 1))

    hi_val = jnp.max(scan(lambda blk, acc: jnp.maximum(acc, jnp.where(blk <= hi, blk, -big)), -big),
                     axis=1, keepdims=True)
    thr, _ = lax.while_loop(walk_cond, walk_body, (hi_val, jnp.int32(1)))
    n_gt = rowsum(scan(lambda blk, acc: acc + jnp.where(blk > thr, 1.0, 0.0), jnp.zeros((nq, LANE), F32)))
    need = topk - n_gt

    m_ref[...] = jnp.full(m_ref.shape, NEG_INF, F32)
    l_ref[...] = jnp.zeros(l_ref.shape, F32)
    acc_ref[...] = jnp.zeros(acc_ref.shape, F32)
    upper = (lax.broadcasted_iota(I32, (LANE, LANE), 0)
             < lax.broadcasted_iota(I32, (LANE, LANE), 1)).astype(BF16)
    qg = [jnp.concatenate([q_ref[:, (g * rep + r) * LANE:(g * rep + r + 1) * LANE]
                           for r in range(rep)], axis=0) for g in range(A_KV_HEADS)]

    def attn_body(t, run):
        off = pl.multiple_of(t * tk, tk)
        keyt = key_ref[:, pl.ds(off, tk)]
        adm = admissible(off)
        sels = []
        for c in range(tk // LANE):
            kc = keyt[:, c * LANE:(c + 1) * LANE]
            eqf = jnp.where(kc == thr, 1.0, 0.0)
            before = _dot(eqf.astype(BF16), upper) + run
            take = jnp.where(kc > thr, 1.0, jnp.where(before < need, eqf, 0.0))
            sels.append(take)
            run = run + jnp.sum(eqf, axis=1, keepdims=True)
        sel = jnp.where(adm, jnp.concatenate(sels, axis=1), 0.0) > 0.5
        sel_r = jnp.concatenate([sel] * rep, axis=0)
        groups = range(A_KV_HEADS)
        ss = [jnp.where(sel_r, _dot_nt(qg[g], k_ref[pl.ds(off, tk), g * LANE:(g + 1) * LANE]), NEG_INF)
              for g in groups]
        m_old = [m_ref[g] for g in groups]
        m_new = [jnp.maximum(m_old[g], _rowmax(ss[g])) for g in groups]
        ps = [jnp.exp2(ss[g] - m_new[g]) for g in groups]
        pv = [_dot(ps[g].astype(BF16), v_ref[pl.ds(off, tk), g * LANE:(g + 1) * LANE]) for g in groups]
        for g in groups:
            alpha = jnp.exp2(m_old[g] - m_new[g])
            l_ref[g] = alpha * l_ref[g] + _rowsum(ps[g])
            acc_ref[g] = alpha * acc_ref[g] + pv[g]
            m_ref[g] = m_new[g]
        return run

    lax.fori_loop(0, n_tiles, attn_body, jnp.zeros((nq, 1), F32))
    for g in range(A_KV_HEADS):
        out = acc_ref[g] / l_ref[g]
        for r in range(rep):
            h = g * rep + r
            o_ref[:, h * LANE:(h + 1) * LANE] = out[r * nq:(r + 1) * nq].astype(o_ref.dtype)


def dsa_attention(q, k, v, iq, ik, iw, *, nq=128, tk=512):
    b, s, _ = q.shape
    tk = min(tk, s)
    topk = min(TOPK_MAX, s // 4)
    rep = A_HEADS // A_KV_HEADS
    qspec = lambda w: pl.BlockSpec((None, nq, w), lambda bi, qi: (bi, qi, 0))
    kspec = lambda w: pl.BlockSpec((None, s, w), lambda bi, qi: (bi, 0, 0))
    return pl.pallas_call(
        functools.partial(_dsa_kernel, tk=tk, topk=topk),
        grid=(b, s // nq),
        in_specs=[qspec(1024), kspec(256), kspec(256), qspec(512), kspec(IDX_DIM), qspec(IDX_HEADS)],
        out_specs=qspec(1024),
        out_shape=jax.ShapeDtypeStruct((b, s, 1024), BF16),
        scratch_shapes=[pltpu.VMEM((nq, s), F32),
                        pltpu.VMEM((A_KV_HEADS, rep * nq, 1), F32),
                        pltpu.VMEM((A_KV_HEADS, rep * nq, 1), F32),
                        pltpu.VMEM((A_KV_HEADS, rep * nq, LANE), F32)],
        compiler_params=_cparams("parallel", "arbitrary"),
        name="dsa_attention",
    )(q, k, v, iq, ik, iw)


def _causal_conv(x, tail_ref, xp_ref, w):
    r = x.shape[0]
    width = w.shape[0]
    xp_ref[0:8, :] = tail_ref[...]
    xp_ref[8:8 + r, :] = x
    tail_ref[...] = x[r - 8:r, :]
    acc = x * w[width - 1:width, :]
    for j in range(width - 1):
        acc = acc + xp_ref[pl.ds(8 - (width - 1) + j, r), :] * w[j:j + 1, :]
    return acc


def _unit_lower_inverse(mats):
    n = mats[0].shape[0]
    eye = (lax.broadcasted_iota(I32, (n, n), 0) == lax.broadcasted_iota(I32, (n, n), 1)).astype(F32)
    ps = [-a for a in mats]
    ts = [eye + p for p in ps]
    for _ in range(int(math.ceil(math.log2(n))) - 1):
        ps = [_bdot(p, p) for p in ps]
        ts = [t + _bdot(t, p) for t, p in zip(ts, ps)]
    return ts


def _gdn_kernel(qkv_ref, z_ref, misc_ref, convw_ref, alog_ref, dtb_ref, nw_ref, o_ref,
                tail_ref, xp_ref, q_s, k_s, v_s, beta_s, g_s, state_ref):
    rows = qkv_ref.shape[0]
    hd = B_HEAD_DIM
    nh = B_HEADS

    @pl.when(pl.program_id(1) == 0)
    def _():
        tail_ref[...] = jnp.zeros(tail_ref.shape, F32)
        state_ref[...] = jnp.zeros(state_ref.shape, F32)

    y = _silu(_causal_conv(qkv_ref[...], tail_ref, xp_ref, convw_ref[...]))
    for h in range(nh):
        qh = y[:, h * hd:(h + 1) * hd]
        kh = y[:, (nh + h) * hd:(nh + h + 1) * hd]
        q_s[:, h * hd:(h + 1) * hd] = qh * lax.rsqrt(jnp.sum(qh * qh, -1, keepdims=True) + 1e-6) * hd ** -0.5
        k_s[:, h * hd:(h + 1) * hd] = kh * lax.rsqrt(jnp.sum(kh * kh, -1, keepdims=True) + 1e-6)
    v_s[...] = y[:, 2 * nh * hd:]
    misc = misc_ref[...]
    beta_s[...] = _sigmoid(misc)
    g_s[...] = -jnp.exp(alog_ref[...]) * _softplus(misc + dtb_ref[...])

    ri = lax.broadcasted_iota(I32, (CHUNK, CHUNK), 0)
    ci = lax.broadcasted_iota(I32, (CHUNK, CHUNK), 1)
    incl = ri >= ci
    strict = ri > ci
    lower01 = incl.astype(BF16)
    nw = nw_ref[...]

    def chunk_body(c, carry):
        r0 = pl.multiple_of(c * CHUNK, CHUNK)
        rs = pl.ds(r0, CHUNK)
        gc = _dot01_left(lower01, g_s[rs, :])
        gct = gc.T
        beta = beta_s[rs, :]
        eg = jnp.exp(gc)
        g_last = gc[CHUNK - 1:CHUNK, :]
        e_last = jnp.exp(g_last)
        e_rem = jnp.exp(g_last - gc)
        heads = range(nh)
        hs = [slice(h * hd, (h + 1) * hd) for h in heads]
        col = lambda x, h: x[:, nh + h:nh + h + 1]
        qh = [q_s[rs, hs[h]] for h in heads]
        kh = [k_s[rs, hs[h]] for h in heads]
        bcol = [beta[:, h:h + 1] for h in heads]
        kb = [kh[h] * bcol[h] for h in heads]
        kk = [_bdot_nt(kb[h], kh[h]) for h in heads]
        qk = [_bdot_nt(qh[h], kh[h]) for h in heads]
        st = [state_ref[h] for h in heads]
        q_st = [_bdot(qh[h] * col(eg, h), st[h]) for h in heads]
        decay = []
        for h in heads:
            diff = col(gc, h) - gct[nh + h:nh + h + 1, :]
            decay.append(jnp.where(incl, jnp.exp(jnp.where(incl, diff, 0.0)), 0.0))
        t_mat = _unit_lower_inverse([jnp.where(strict, kk[h] * decay[h], 0.0) for h in heads])
        uw = [_bdot(t_mat[h], jnp.concatenate([v_s[rs, hs[h]] * bcol[h], kb[h] * col(eg, h)], axis=1))
              for h in heads]
        w_st = [_bdot(uw[h][:, hd:], st[h]) for h in heads]
        v_new = [uw[h][:, :hd] - w_st[h] for h in heads]
        o = [q_st[h] + _bdot(qk[h] * decay[h], v_new[h]) for h in heads]
        upd = [_bdot((kh[h] * col(e_rem, h)).T, v_new[h]) for h in heads]
        for h in heads:
            state_ref[h] = st[h] * col(e_last, h) + upd[h]
            on = o[h] * lax.rsqrt(jnp.mean(o[h] * o[h], -1, keepdims=True) + NORM_EPS) * nw
            o_ref[rs, hs[h]] = (on * _silu(z_ref[rs, hs[h]])).astype(o_ref.dtype)
        return carry

    lax.fori_loop(0, rows // CHUNK, chunk_body, 0)


def gdn_mixer(proj, conv_w, alog_row, dtb_row, norm_w, *, batch, rows=256):
    t = proj.shape[0]
    s = t // batch
    rows = min(rows, s)
    nr = s // rows
    nqkv = 3 * B_HEADS * B_HEAD_DIM
    nz = B_HEADS * B_HEAD_DIM
    const = lambda shape: pl.BlockSpec(shape, lambda b, r: (0, 0))
    return pl.pallas_call(
        _gdn_kernel,
        grid=(batch, nr),
        in_specs=[pl.BlockSpec((rows, nqkv), lambda b, r: (b * nr + r, 0)),
                  pl.BlockSpec((rows, nz), lambda b, r: (b * nr + r, nqkv // nz)),
                  pl.BlockSpec((rows, LANE), lambda b, r: (b * nr + r, (nqkv + nz) // LANE)),
                  const((CONV_WIDTH, nqkv)), const((1, LANE)), const((1, LANE)), const((1, LANE))],
        out_specs=pl.BlockSpec((rows, nz), lambda b, r: (b * nr + r, 0)),
        out_shape=jax.ShapeDtypeStruct((t, nz), BF16),
        scratch_shapes=[pltpu.VMEM((8, nqkv), F32), pltpu.VMEM((rows + 8, nqkv), F32),
                        pltpu.VMEM((rows, nz), F32), pltpu.VMEM((rows, nz), F32), pltpu.VMEM((rows, nz), F32),
                        pltpu.VMEM((rows, LANE), F32), pltpu.VMEM((rows, LANE), F32),
                        pltpu.VMEM((B_HEADS, B_HEAD_DIM, B_HEAD_DIM), F32)],
        compiler_params=_cparams("parallel", "arbitrary"),
        name="gdn",
    )(proj, proj, proj, conv_w, alog_row, dtb_row, norm_w)


def _mla_prep_kernel(q_ref, kr_ref, ci_ref, slo_ref, shi_ref, qo_ref, kro_ref):
    ci, slo, shi = ci_ref[...], slo_ref[...], shi_ref[...]
    scale = (C_NOPE + C_ROPE) ** -0.5 * math.log2(math.e)
    for h in range(C_HEADS):
        base = h * 2 * LANE
        qo_ref[:, base:base + LANE] = (q_ref[:, base:base + LANE] * scale).astype(BF16)
        qo_ref[:, base + LANE:base + 2 * LANE] = (
            _rope64(q_ref[:, base + LANE:base + 2 * LANE], ci, slo, shi) * scale).astype(BF16)
    kro_ref[...] = _rope64(kr_ref[...], ci, slo, shi).astype(BF16)


def mla_prep(q_raw, proj_c, tabs, *, tm=512):
    t, n = q_raw.shape
    tm = min(tm, t)
    row = lambda w: pl.BlockSpec((tm, w), lambda i: (i, 0))
    return pl.pallas_call(
        _mla_prep_kernel,
        grid=(t // tm,),
        in_specs=[row(n), pl.BlockSpec((tm, LANE), lambda i: (i, (C_Q_RANK + C_KV_RANK) // LANE)),
                  row(LANE), row(LANE), row(LANE)],
        out_specs=[row(n), row(LANE)],
        out_shape=[jax.ShapeDtypeStruct((t, n), BF16), jax.ShapeDtypeStruct((t, LANE), BF16)],
        compiler_params=_cparams("parallel"),
        name="mla_prep",
    )(q_raw, proj_c, *tabs)


def _mla_kernel(q_ref, kv_ref, kr_ref, o_ref, m_ref, l_ref, acc_ref, *, hb):
    qi = pl.program_id(2)
    tq = q_ref.shape[0]
    heads = range(hb)
    row = lax.broadcasted_iota(I32, (tq, tq), 0)
    col = lax.broadcasted_iota(I32, (tq, tq), 1)
    diag_mask = (col // CHUNK) <= (row // CHUNK)
    m_ref[...] = jnp.full(m_ref.shape, NEG_INF, F32)
    l_ref[...] = jnp.zeros(l_ref.shape, F32)
    acc_ref[...] = jnp.zeros(acc_ref.shape, F32)
    qs = [q_ref[:, h * 2 * LANE:(h + 1) * 2 * LANE] for h in heads]

    def step(off, on_diagonal):
        rows = pl.ds(off, tq)
        kr = kr_ref[rows, :]
        ss = [_dot_nt(qs[h], jnp.concatenate([kv_ref[rows, h * 2 * LANE:h * 2 * LANE + LANE], kr], axis=1))
              for h in heads]
        if on_diagonal:
            ss = [jnp.where(diag_mask, s, NEG_INF) for s in ss]
        m_old = [m_ref[h] for h in heads]
        m_new = [jnp.maximum(m_old[h], _rowmax(ss[h])) for h in heads]
        ps = [jnp.exp2(ss[h] - m_new[h]) for h in heads]
        pv = [_dot(ps[h].astype(BF16), kv_ref[rows, h * 2 * LANE + LANE:(h + 1) * 2 * LANE]) for h in heads]
        for h in heads:
            alpha = jnp.exp2(m_old[h] - m_new[h])
            l_ref[h] = alpha * l_ref[h] + _rowsum(ps[h])
            acc_ref[h] = alpha * acc_ref[h] + pv[h]
            m_ref[h] = m_new[h]

    def body(t, carry):
        step(pl.multiple_of(t * tq, tq), False)
        return carry

    lax.fori_loop(0, qi, body, 0)
    step(pl.multiple_of(qi * tq, tq), True)
    for h in heads:
        o_ref[:, h * LANE:(h + 1) * LANE] = (acc_ref[h] / l_ref[h]).astype(o_ref.dtype)


def mla_attention(q, kv, kr, *, tq=512, hb=4):
    b, s, _ = q.shape
    tq = min(tq, s)
    return pl.pallas_call(
        functools.partial(_mla_kernel, hb=hb),
        grid=(b, C_HEADS // hb, s // tq),
        in_specs=[pl.BlockSpec((None, tq, hb * 2 * LANE), lambda bi, h, qi: (bi, qi, h)),
                  pl.BlockSpec((None, s, hb * 2 * LANE), lambda bi, h, qi: (bi, 0, h)),
                  pl.BlockSpec((None, s, LANE), lambda bi, h, qi: (bi, 0, 0))],
        out_specs=pl.BlockSpec((None, tq, hb * LANE), lambda bi, h, qi: (bi, qi, h)),
        out_shape=jax.ShapeDtypeStruct((b, s, C_HEADS * C_V), BF16),
        scratch_shapes=[pltpu.VMEM((hb, tq, 1), F32), pltpu.VMEM((hb, tq, 1), F32),
                        pltpu.VMEM((hb, tq, LANE), F32)],
        compiler_params=_cparams("parallel", "parallel", "arbitrary"),
        name="mla_attention",
    )(q, kv, kr)


def _ssd_kernel(zx_ref, misc_ref, convw_ref, convb_ref, dtb_ref, a_ref, aexp_ref, dskip_ref,
                nw_ref, expand_ref, o_ref,
                tail_ref, xp_ref, xs_s, xdt_s, bm_s, cm_s, la_s, laexp_s, y_s, state_ref):
    rows = zx_ref.shape[0]
    pd = D_HEAD_DIM
    gw = D_INNER // D_GROUPS
    pairs_per_group = gw // LANE

    @pl.when(pl.program_id(1) == 0)
    def _():
        tail_ref[...] = jnp.zeros(tail_ref.shape, F32)
        state_ref[...] = jnp.zeros(state_ref.shape, F32)

    y = _silu(_causal_conv(zx_ref[:, D_INNER:], tail_ref, xp_ref, convw_ref[...]) + convb_ref[...])
    xs = y[:, :D_INNER]
    xs_s[...] = xs
    bm_s[...] = y[:, D_INNER:D_INNER + D_GROUPS * D_STATE]
    cm_s[...] = y[:, D_INNER + D_GROUPS * D_STATE:]
    dt = _softplus(misc_ref[...] + dtb_ref[...])
    dt_exp = _dot01_right(dt, expand_ref[...])
    xdt_s[...] = xs * dt_exp
    la_s[...] = dt * a_ref[...]
    laexp_s[...] = dt_exp * aexp_ref[...]

    ri = lax.broadcasted_iota(I32, (CHUNK, CHUNK), 0)
    ci = lax.broadcasted_iota(I32, (CHUNK, CHUNK), 1)
    incl = ri >= ci
    lower01 = incl.astype(BF16)
    lane_lo = lax.broadcasted_iota(I32, (CHUNK, LANE), 1) < pd

    def chunk_body(c, carry):
        r0 = pl.multiple_of(c * CHUNK, CHUNK)
        rs = pl.ds(r0, CHUNK)
        cum = _dot01_left(lower01, la_s[rs, :])
        cumt = cum.T
        cum_e = _dot01_left(lower01, laexp_s[rs, :])
        ecum = jnp.exp(cum_e)
        last = cum_e[CHUNK - 1:CHUNK, :]
        dec_st = jnp.exp(last - cum_e)
        cdec = jnp.exp(last)
        for g in range(D_GROUPS):
            bm = bm_s[rs, g * D_STATE:(g + 1) * D_STATE]
            cm = cm_s[rs, g * D_STATE:(g + 1) * D_STATE]
            cb = _bdot_nt(cm, bm)
            bmt = bm.T.astype(BF16)
            gs = slice(g * gw, (g + 1) * gw)
            y_off = _bdot(cm, state_ref[:, gs]) * ecum[:, gs]
            for pp in range(pairs_per_group):
                pidx = g * pairs_per_group + pp
                ps = slice(pidx * LANE, (pidx + 1) * LANE)
                xp = xdt_s[rs, ps]
                ys = []
                for hh in range(2):
                    h = 2 * pidx + hh
                    diff = cum[:, h:h + 1] - cumt[h:h + 1, :]
                    lmat = jnp.where(incl, jnp.exp(jnp.where(incl, diff, 0.0)), 0.0)
                    ys.append(_bdot(cb * lmat, xp))
                y_diag = jnp.where(lane_lo, ys[0], ys[1])
                y_s[rs, ps] = (y_diag + y_off[:, pp * LANE:(pp + 1) * LANE]
                               + dskip_ref[:, ps] * xs_s[rs, ps])
                s_new = _dot(bmt, (xp * dec_st[:, ps]).astype(BF16))
                state_ref[:, ps] = state_ref[:, ps] * cdec[:, ps] + s_new
        return carry

    lax.fori_loop(0, rows // CHUNK, chunk_body, 0)
    yv = y_s[...] * _silu(zx_ref[:, :D_INNER])
    for g in range(D_GROUPS):
        gs = slice(g * gw, (g + 1) * gw)
        yg = yv[:, gs]
        ms = jnp.mean(yg * yg, axis=-1, keepdims=True)
        o_ref[:, gs] = (yg * lax.rsqrt(ms + NORM_EPS) * nw_ref[:, gs]).astype(o_ref.dtype)


def ssd_mixer(proj, conv_w, conv_b, dtb_row, a_row, a_exp, dskip_exp, norm_w, expand, *, batch, rows=256):
    t = proj.shape[0]
    s = t // batch
    rows = min(rows, s)
    nr = s // rows
    nx = D_INNER + 2 * D_GROUPS * D_STATE
    const = lambda shape: pl.BlockSpec(shape, lambda b, r: (0, 0))
    return pl.pallas_call(
        _ssd_kernel,
        grid=(batch, nr),
        in_specs=[pl.BlockSpec((rows, D_INNER + nx), lambda b, r: (b * nr + r, 0)),
                  pl.BlockSpec((rows, LANE), lambda b, r: (b * nr + r, (D_INNER + nx) // LANE)),
                  const((CONV_WIDTH, nx)), const((1, nx)), const((1, LANE)), const((1, LANE)),
                  const((1, D_INNER)), const((1, D_INNER)), const((1, D_INNER)), const((LANE, D_INNER))],
        out_specs=pl.BlockSpec((rows, D_INNER), lambda b, r: (b * nr + r, 0)),
        out_shape=jax.ShapeDtypeStruct((t, D_INNER), BF16),
        scratch_shapes=[pltpu.VMEM((8, nx), F32), pltpu.VMEM((rows + 8, nx), F32),
                        pltpu.VMEM((rows, D_INNER), F32), pltpu.VMEM((rows, D_INNER), F32),
                        pltpu.VMEM((rows, D_GROUPS * D_STATE), F32), pltpu.VMEM((rows, D_GROUPS * D_STATE), F32),
                        pltpu.VMEM((rows, LANE), F32), pltpu.VMEM((rows, D_INNER), F32),
                        pltpu.VMEM((rows, D_INNER), F32),
                        pltpu.VMEM((D_STATE, D_INNER), F32)],
        compiler_params=_cparams("parallel", "arbitrary"),
        name="ssd",
    )(proj, proj, conv_w, conv_b, dtb_row, a_row, a_exp, dskip_exp, norm_w, expand)


def _rope_tables(positions):
    pos = positions.reshape(-1).astype(F32)[:, None]

    def cs(dim):
        inv = jnp.power(ROPE_THETA, -jnp.arange(0, dim, 2, dtype=F32) / dim)
        ang = pos * inv
        return jnp.cos(ang), jnp.sin(ang)

    c, s = cs(A_HEAD_DIM)
    tab_a = (jnp.concatenate([c, c], -1), jnp.concatenate([-s, s], -1))
    c, s = cs(IDX_DIM)
    z = jnp.zeros_like(s)
    tab_i = (jnp.concatenate([c, c, c, c], -1), jnp.concatenate([-s, z, -s, z], -1),
             jnp.concatenate([z, s, z, s], -1))
    return tab_a, tab_i


def _pad_cols(w, n):
    return jnp.pad(w, ((0, 0), (0, n - w.shape[1])))


def _lane_row(v, start):
    return jnp.zeros((1, LANE), F32).at[0, start:start + v.shape[0]].set(v.astype(F32))


def _moe_weights(i, w_group, b_group, w_expert, b_expert, w_gate, w_up, w_down):
    w_router = _pad_cols(jnp.concatenate([w_group[i], w_expert[i]], axis=1), LANE).astype(F32)
    b_router = _pad_cols(jnp.concatenate([b_group[i], b_expert[i]])[None], LANE).astype(F32)
    return w_router, b_router, w_gate[i], w_up[i], w_down[i]


def kernel(x, p, positions, norm_mix, norm_ffn, norm_ple, norm_final, ev_w_in, ev_w_out, gdn_conv_w,
           gdn_a_log, gdn_dt_bias, gdn_norm, od_w_in, od_w_out, mla_q_norm, mla_kv_norm, mla_w_uq,
           mla_w_ukv, ssm_conv_w, ssm_conv_b, ssm_a_log, ssm_dt_bias, ssm_d_skip, ssm_norm, moe_w_group,
           moe_b_group, moe_w_expert, moe_b_expert, moe_w_gate, moe_w_up, moe_w_down, ple_w_proj,
           ple_w_gate):
    b, s, d = x.shape
    t = b * s
    tab_a, tab_i = _rope_tables(positions)
    h = x.reshape(t, d)
    moe_args = (moe_w_group, moe_b_group, moe_w_expert, moe_b_expert, moe_w_gate, moe_w_up, moe_w_down)

    w = ev_w_in[0]
    o = np_cumsum((1024, 256, 256, 512, 64, 8, 3072, 1024, 8, 8))
    w_a = _pad_cols(w[:, :o[6]], 2176).astype(BF16)
    w_b = _pad_cols(jnp.concatenate([w[:, o[6]:o[8]], w[:, o[8]:]], axis=1), 4224).astype(BF16)
    proj_a = norm_matmul(h, norm_mix[0], w_a, k=d)
    proj_b = norm_matmul(h, norm_mix[0], w_b, k=d, tn=1408)
    q, k, v, iq, ik, iw = dsa_prep(proj_a, tab_a + tab_i)
    r3 = lambda a: a.reshape(b, s, a.shape[-1])
    o_a = dsa_attention(r3(q), r3(k), r3(v), r3(iq), r3(ik), r3(iw)).reshape(t, -1)
    o_b = gdn_mixer(proj_b, gdn_conv_w[0].astype(F32), _lane_row(gdn_a_log[0], B_HEADS),
                    _lane_row(gdn_dt_bias[0], B_HEADS), gdn_norm[0].reshape(1, -1).astype(F32), batch=b)
    h = outproj_residual(o_a, o_b, ev_w_out[0].astype(BF16), h)
    h = moe_layer(h, norm_ffn[0], *_moe_weights(0, *moe_args))
    h = ple_residual(h, norm_ple[0], ple_w_gate[0].astype(BF16), p[0].reshape(t, -1),
                     ple_w_proj[0].astype(BF16))

    w = od_w_in[0]
    o = np_cumsum((512, 512, 64, 1024, 1536, 16))
    w_c = _pad_cols(w[:, :o[3]], 1152).astype(BF16)
    w_d = _pad_cols(w[:, o[3]:], 2688).astype(BF16)
    proj_c = norm_matmul(h, norm_mix[1], w_c, k=d)
    proj_d = norm_matmul(h, norm_mix[1], w_d, k=d, tn=896)
    w_uq = jnp.pad(mla_w_uq[0].reshape(C_Q_RANK, C_HEADS, C_NOPE + C_ROPE),
                   ((0, 0), (0, 0), (0, 2 * LANE - C_NOPE - C_ROPE))).reshape(C_Q_RANK, -1).astype(BF16)
    q_raw = norm_matmul(proj_c, mla_q_norm[0], w_uq, k=C_Q_RANK, xblk=0, tn=1024)
    kv = norm_matmul(proj_c, mla_kv_norm[0], mla_w_ukv[0].astype(BF16), k=C_KV_RANK, xblk=1, tn=1024,
                     out_dtype=BF16)
    q, kr = mla_prep(q_raw, proj_c, tab_i)
    o_c = mla_attention(r3(q), r3(kv), r3(kr)).reshape(t, -1)
    a_heads = -jnp.exp(ssm_a_log[0].astype(F32))
    expand = (jnp.arange(LANE)[:, None] == (jnp.arange(D_INNER)[None, :] // D_HEAD_DIM)).astype(BF16)
    o_d = ssd_mixer(proj_d, ssm_conv_w[0].astype(F32), ssm_conv_b[0].reshape(1, -1).astype(F32),
                    _lane_row(ssm_dt_bias[0], 0), _lane_row(a_heads, 0),
                    jnp.repeat(a_heads, D_HEAD_DIM)[None], jnp.repeat(ssm_d_skip[0].astype(F32), D_HEAD_DIM)[None],
                    ssm_norm[0].reshape(1, -1).astype(F32), expand, batch=b)
    h = outproj_residual(o_c, o_d, od_w_out[0].astype(BF16), h)
    h = moe_layer(h, norm_ffn[1], *_moe_weights(1, *moe_args))
    h = ple_residual(h, norm_ple[1], ple_w_gate[1].astype(BF16), p[1].reshape(t, -1),
                     ple_w_proj[1].astype(BF16))
    return rmsnorm(h, norm_final).reshape(b, s, d)


def np_cumsum(sizes):
    out, acc = [], 0
    for v in sizes:
        out.append(acc)
        acc += v
    return out
```

```python
import functools
import math

import jax
import jax.numpy as jnp
from jax import lax
from jax.experimental import pallas as pl
from jax.experimental.pallas import tpu as pltpu

F32 = jnp.float32
BF16 = jnp.bfloat16
I32 = jnp.int32

CHUNK = 64
ROPE_THETA = 10000.0
NORM_EPS = 1e-6
NEG_INF = -1e30
BISECT_STEPS = 24

A_HEADS, A_KV_HEADS, A_HEAD_DIM = 8, 2, 128
IDX_HEADS, IDX_DIM = 8, 64
IDX_SCALE = (IDX_HEADS * IDX_DIM) ** -0.5
TOPK_MAX = 256
B_HEADS, B_HEAD_DIM, CONV_WIDTH = 8, 128, 4
C_HEADS, C_Q_RANK, C_KV_RANK, C_NOPE, C_ROPE, C_V = 8, 512, 512, 128, 64, 128
D_INNER, D_HEAD_DIM, D_GROUPS, D_STATE = 1024, 64, 2, 128
D_HEADS = D_INNER // D_HEAD_DIM
N_GROUPS, EXPERTS_PER_GROUP, EXPERT_FF = 4, 8, 256
N_EXPERTS = N_GROUPS * EXPERTS_PER_GROUP

LANE = 128
VMEM_BYTES_V7X = 64 * 1024 * 1024
VMEM_LIMIT = VMEM_BYTES_V7X - 8 * 1024 * 1024


def _cparams(*sem):
    return pltpu.CompilerParams(dimension_semantics=sem, vmem_limit_bytes=VMEM_LIMIT)


def _dot(a, b):
    return jnp.dot(a, b, preferred_element_type=F32)


def _dot_nt(a, b):
    return lax.dot_general(a, b, (((1,), (1,)), ((), ())), preferred_element_type=F32)


def _bdot(a, b):
    return _dot(a.astype(BF16), b.astype(BF16))


def _bdot_nt(a, b):
    return _dot_nt(a.astype(BF16), b.astype(BF16))


def _split3(x):
    hi = x.astype(BF16)
    r1 = x - hi.astype(F32)
    mid = r1.astype(BF16)
    lo = (r1 - mid.astype(F32)).astype(BF16)
    return hi, mid, lo


def _dot01_left(m01, x):
    hi, mid, lo = _split3(x)
    return _dot(m01, hi) + _dot(m01, mid) + _dot(m01, lo)


def _dot01_right(x, m01):
    hi, mid, lo = _split3(x)
    return _dot(hi, m01) + _dot(mid, m01) + _dot(lo, m01)


def _sigmoid(x):
    return 1.0 / (1.0 + jnp.exp(-x))


def _silu(x):
    return x * _sigmoid(x)


def _softplus(x):
    return jnp.maximum(x, 0.0) + jnp.log1p(jnp.exp(-jnp.abs(x)))


def _lane_fold(x, op):
    blocks = [x[:, c * LANE:(c + 1) * LANE] for c in range(x.shape[1] // LANE)]
    while len(blocks) > 1:
        blocks = [op(blocks[i], blocks[i + 1]) for i in range(0, len(blocks), 2)]
    return blocks[0]


def _rowmax(x):
    return jnp.max(_lane_fold(x, jnp.maximum), axis=1, keepdims=True)


def _rowsum(x):
    return jnp.sum(_lane_fold(x, jnp.add), axis=1, keepdims=True)


def _rope64(x, c, slo, shi):
    return x * c + pltpu.roll(x, 96, 1) * slo + pltpu.roll(x, 32, 1) * shi


def _rope128(x, c, s):
    return x * c + pltpu.roll(x, 64, 1) * s


def _norm_matmul_kernel(x_ref, g_ref, w_ref, o_ref, xn_ref):
    @pl.when(pl.program_id(1) == 0)
    def _():
        x = x_ref[...].astype(F32)
        ms = jnp.mean(x * x, axis=-1, keepdims=True)
        xn_ref[...] = (x * lax.rsqrt(ms + NORM_EPS) * g_ref[...]).astype(BF16)

    o_ref[...] = _dot(xn_ref[...], w_ref[...]).astype(o_ref.dtype)


def norm_matmul(x, g, w, *, k, xblk=0, tm=512, tn=None, out_dtype=F32):
    t = x.shape[0]
    n = w.shape[1]
    tn = n if tn is None else tn
    tm = min(tm, t)
    return pl.pallas_call(
        _norm_matmul_kernel,
        grid=(t // tm, n // tn),
        in_specs=[pl.BlockSpec((tm, k), lambda i, j: (i, xblk)),
                  pl.BlockSpec((1, k), lambda i, j: (0, 0)),
                  pl.BlockSpec((k, tn), lambda i, j: (0, j))],
        out_specs=pl.BlockSpec((tm, tn), lambda i, j: (i, j)),
        out_shape=jax.ShapeDtypeStruct((t, n), out_dtype),
        scratch_shapes=[pltpu.VMEM((tm, k), BF16)],
        compiler_params=_cparams("parallel", "arbitrary"),
        name="norm_matmul",
    )(x, g.reshape(1, k).astype(F32), w)


def _outproj_kernel(a1_ref, a2_ref, w1_ref, w2_ref, r_ref, o_ref):
    o_ref[...] = r_ref[...] + _dot(a1_ref[...], w1_ref[...]) + _dot(a2_ref[...], w2_ref[...])


def outproj_residual(a1, a2, w, res, *, tm=512, tn=1024):
    t, kh = a1.shape
    n = w.shape[1]
    tm = min(tm, t)
    return pl.pallas_call(
        _outproj_kernel,
        grid=(t // tm, n // tn),
        in_specs=[pl.BlockSpec((tm, kh), lambda i, j: (i, 0)),
                  pl.BlockSpec((tm, kh), lambda i, j: (i, 0)),
                  pl.BlockSpec((kh, tn), lambda i, j: (0, j)),
                  pl.BlockSpec((kh, tn), lambda i, j: (1, j)),
                  pl.BlockSpec((tm, tn), lambda i, j: (i, j))],
        out_specs=pl.BlockSpec((tm, tn), lambda i, j: (i, j)),
        out_shape=jax.ShapeDtypeStruct((t, n), F32),
        compiler_params=_cparams("parallel", "arbitrary"),
        name="outproj",
    )(a1, a2, w, w, res)


def _ple_kernel(h_ref, g_ref, wg_ref, p_ref, wp_ref, hres_ref, o_ref, xn_ref):
    @pl.when(pl.program_id(1) == 0)
    def _():
        x = h_ref[...]
        ms = jnp.mean(x * x, axis=-1, keepdims=True)
        xn_ref[...] = (x * lax.rsqrt(ms + NORM_EPS) * g_ref[...]).astype(BF16)

    gate = _sigmoid(_dot(xn_ref[...], wg_ref[...]))
    emb = _dot(p_ref[...].astype(BF16), wp_ref[...])
    o_ref[...] = hres_ref[...] + gate * emb


def ple_residual(h, g, wg, p, wp, *, tm=512, tn=1024):
    t, d = h.shape
    pd = p.shape[1]
    tm = min(tm, t)
    return pl.pallas_call(
        _ple_kernel,
        grid=(t // tm, d // tn),
        in_specs=[pl.BlockSpec((tm, d), lambda i, j: (i, 0)),
                  pl.BlockSpec((1, d), lambda i, j: (0, 0)),
                  pl.BlockSpec((d, tn), lambda i, j: (0, j)),
                  pl.BlockSpec((tm, pd), lambda i, j: (i, 0)),
                  pl.BlockSpec((pd, tn), lambda i, j: (0, j)),
                  pl.BlockSpec((tm, tn), lambda i, j: (i, j))],
        out_specs=pl.BlockSpec((tm, tn), lambda i, j: (i, j)),
        out_shape=jax.ShapeDtypeStruct((t, d), F32),
        scratch_shapes=[pltpu.VMEM((tm, d), BF16)],
        compiler_params=_cparams("parallel", "arbitrary"),
        name="ple",
    )(h, g.reshape(1, d).astype(F32), wg, p, wp, h)


def _rmsnorm_kernel(x_ref, g_ref, o_ref):
    x = x_ref[...]
    ms = jnp.mean(x * x, axis=-1, keepdims=True)
    o_ref[...] = x * lax.rsqrt(ms + NORM_EPS) * g_ref[...]


def rmsnorm(x, g, *, tm=512):
    t, d = x.shape
    tm = min(tm, t)
    return pl.pallas_call(
        _rmsnorm_kernel,
        grid=(t // tm,),
        in_specs=[pl.BlockSpec((tm, d), lambda i: (i, 0)),
                  pl.BlockSpec((1, d), lambda i: (0, 0))],
        out_specs=pl.BlockSpec((tm, d), lambda i: (i, 0)),
        out_shape=jax.ShapeDtypeStruct((t, d), F32),
        compiler_params=_cparams("parallel"),
        name="final_norm",
    )(x, g.reshape(1, d).astype(F32))


def _rms_normed(h_ref, g_ref):
    x = h_ref[...]
    ms = jnp.mean(x * x, axis=-1, keepdims=True)
    return x * lax.rsqrt(ms + NORM_EPS) * g_ref[...]


def _route(xn, wr, br):
    n = xn.shape[0]
    x_hi, x_lo, _ = _split3(xn)
    w_hi, w_lo, _ = _split3(wr)
    logits = _dot(x_hi, w_hi) + (_dot(x_hi, w_lo) + _dot(x_lo, w_hi)) + br
    lane = lax.broadcasted_iota(I32, (n, LANE), 1)
    glog = jnp.where(lane < N_GROUPS, logits, NEG_INF)
    gmax = jnp.max(glog, axis=-1, keepdims=True)
    g_sel = jnp.min(jnp.where(glog == gmax, lane, LANE), axis=-1, keepdims=True)
    p_sel = 1.0 / jnp.sum(jnp.exp(glog - gmax), axis=-1, keepdims=True)
    eidx = lane - N_GROUPS
    in_grp = (eidx >= 0) & (eidx < N_EXPERTS) & ((eidx // EXPERTS_PER_GROUP) == g_sel)
    e1 = jnp.where(in_grp, logits, NEG_INF)
    v1 = jnp.max(e1, axis=-1, keepdims=True)
    i1 = jnp.min(jnp.where(e1 == v1, eidx, LANE), axis=-1, keepdims=True)
    e2 = jnp.where(eidx == i1, NEG_INF, e1)
    v2 = jnp.max(e2, axis=-1, keepdims=True)
    i2 = jnp.min(jnp.where(e2 == v2, eidx, LANE), axis=-1, keepdims=True)
    ex = jnp.exp(v2 - v1)
    return i1, i2, p_sel / (1.0 + ex), p_sel * ex / (1.0 + ex)


def _moe_kernel(h_ref, g_ref, wr_ref, br_ref, wgu_ref, wd_ref, o_ref, xn_ref, comb_ref, *, ke):
    e = pl.program_id(1)
    tm = h_ref.shape[0]

    @pl.when(e == 0)
    def _():
        xn = _rms_normed(h_ref, g_ref)
        xn_ref[...] = xn.astype(BF16)
        o_ref[...] = h_ref[...]
        i1, i2, w1, w2 = _route(xn, wr_ref[...], br_ref[...])
        lane = lax.broadcasted_iota(I32, (tm, LANE), 1)
        comb_ref[...] = jnp.where(lane == i1, w1, 0.0) + jnp.where(lane == i2, w2, 0.0)

    xn = xn_ref[...]
    lane = lax.broadcasted_iota(I32, (tm, LANE), 1)
    comb = comb_ref[...]
    hids = []
    for j in range(ke):
        gu = _dot(xn, wgu_ref[j])
        c = jnp.sum(jnp.where(lane == e * ke + j, comb, 0.0), axis=-1, keepdims=True)
        hid = _silu(gu[:, :EXPERT_FF]) * gu[:, EXPERT_FF:] * c
        hids.append(hid.astype(BF16))
    hid = jnp.concatenate(hids, axis=1)
    wd = wd_ref[...].reshape(ke * EXPERT_FF, wd_ref.shape[2])
    o_ref[...] += _dot(hid, wd)


def moe_residual(h, g, w_router, b_router, w_gu, w_down, *, tm=512, ke=4):
    t, d = h.shape
    tm = min(tm, t)
    return pl.pallas_call(
        functools.partial(_moe_kernel, ke=ke),
        grid=(t // tm, N_EXPERTS // ke),
        in_specs=[pl.BlockSpec((tm, d), lambda i, e: (i, 0)),
                  pl.BlockSpec((1, d), lambda i, e: (0, 0)),
                  pl.BlockSpec((d, LANE), lambda i, e: (0, 0)),
                  pl.BlockSpec((1, LANE), lambda i, e: (0, 0)),
                  pl.BlockSpec((ke, d, 2 * EXPERT_FF), lambda i, e: (e, 0, 0)),
                  pl.BlockSpec((ke, EXPERT_FF, d), lambda i, e: (e, 0, 0))],
        out_specs=pl.BlockSpec((tm, d), lambda i, e: (i, 0)),
        out_shape=jax.ShapeDtypeStruct((t, d), F32),
        scratch_shapes=[pltpu.VMEM((tm, d), BF16), pltpu.VMEM((tm, LANE), F32)],
        compiler_params=_cparams("parallel", "arbitrary"),
        name="moe",
    )(h, g.reshape(1, d).astype(F32), w_router, b_router, w_gu, w_down)


MOE_SUB = 256
MOE_CAP = 48
MOE_SLOTS = N_EXPERTS * MOE_CAP


def _moe_gather_kernel(h_ref, g_ref, wr_ref, br_ref, xc_ref, meta_ref, flag_ref):
    n = h_ref.shape[0]
    xn = _rms_normed(h_ref, g_ref)
    i1, i2, w1, w2 = _route(xn, wr_ref[...], br_ref[...])
    lane = lax.broadcasted_iota(I32, (n, LANE), 1)
    onehot = jnp.where((lane == i1) | (lane == i2), 1.0, 0.0)
    earlier = (lax.broadcasted_iota(I32, (n, n), 1) < lax.broadcasted_iota(I32, (n, n), 0)).astype(BF16)
    rank = _dot(earlier, onehot.astype(BF16))
    r1 = jnp.sum(jnp.where(lane == i1, rank, 0.0), axis=-1, keepdims=True)
    r2 = jnp.sum(jnp.where(lane == i2, rank, 0.0), axis=-1, keepdims=True)
    fits = (r1 < MOE_CAP) & (r2 < MOE_CAP)
    flag_ref[...] = jnp.broadcast_to(jnp.max(jnp.where(fits, 0.0, 1.0), axis=0, keepdims=True), flag_ref.shape)
    pos1 = jnp.where(r1 < MOE_CAP, i1.astype(F32) * MOE_CAP + r1, -1.0)
    pos2 = jnp.where(r2 < MOE_CAP, i2.astype(F32) * MOE_CAP + r2, -1.0)
    meta = (jnp.where(lane == 0, pos1, 0.0) + jnp.where(lane == 1, pos2, 0.0)
            + jnp.where(lane == 2, w1, 0.0) + jnp.where(lane == 3, w2, 0.0))
    meta_ref[...] = meta
    meta_t = meta.T
    slot = lax.broadcasted_iota(I32, (MOE_SLOTS, n), 0).astype(F32)
    pick = jnp.where((slot == meta_t[0:1, :]) | (slot == meta_t[1:2, :]), 1.0, 0.0).astype(BF16)
    xc = _dot(pick, xn.astype(BF16)).astype(BF16)
    xc_ref[...] = xc.reshape(xc_ref.shape)


def _moe_ffn_kernel(x_ref, wg_ref, wu_ref, wd_ref, y_ref):
    x = x_ref[...]
    hid = _silu(_dot(x, wg_ref[...].astype(BF16))) * _dot(x, wu_ref[...].astype(BF16))
    y_ref[...] = _dot(hid.astype(BF16), wd_ref[...].astype(BF16)).astype(y_ref.dtype)


def _moe_scatter_kernel(y_ref, meta_ref, h_ref, o_ref):
    n = h_ref.shape[0]
    meta = meta_ref[...]
    slot = lax.broadcasted_iota(I32, (n, MOE_SLOTS), 1).astype(F32)
    y = y_ref[...].reshape(MOE_SLOTS, y_ref.shape[-1])
    pick = (jnp.where(slot == meta[:, 0:1], meta[:, 2:3], 0.0)
            + jnp.where(slot == meta[:, 1:2], meta[:, 3:4], 0.0)).astype(BF16)
    o_ref[...] = h_ref[...] + _dot(pick, y)


def moe_routed(h, g, w_router, b_router, w_gate, w_up, w_down, layer, *, tm=1024):
    t, d = h.shape
    nsub = t // MOE_SUB
    rows = nsub * MOE_CAP
    tm = min(tm, rows)
    xc, meta, flag = pl.pallas_call(
        _moe_gather_kernel,
        grid=(nsub,),
        in_specs=[pl.BlockSpec((MOE_SUB, d), lambda i: (i, 0)),
                  pl.BlockSpec((1, d), lambda i: (0, 0)),
                  pl.BlockSpec((d, LANE), lambda i: (0, 0)),
                  pl.BlockSpec((1, LANE), lambda i: (0, 0))],
        out_specs=[pl.BlockSpec((N_EXPERTS, None, MOE_CAP, d), lambda i: (0, i, 0, 0)),
                   pl.BlockSpec((MOE_SUB, LANE), lambda i: (i, 0)),
                   pl.BlockSpec((None, 8, LANE), lambda i: (i, 0, 0))],
        out_shape=[jax.ShapeDtypeStruct((N_EXPERTS, nsub, MOE_CAP, d), BF16),
                   jax.ShapeDtypeStruct((t, LANE), F32),
                   jax.ShapeDtypeStruct((nsub, 8, LANE), F32)],
        compiler_params=_cparams("parallel"),
        name="moe_gather",
    )(h, g.reshape(1, d).astype(F32), w_router, b_router)
    y = pl.pallas_call(
        _moe_ffn_kernel,
        grid=(N_EXPERTS, rows // tm),
        in_specs=[pl.BlockSpec((None, tm, d), lambda e, i: (e, i, 0)),
                  pl.BlockSpec((None, None, d, EXPERT_FF), lambda e, i: (layer, e, 0, 0)),
                  pl.BlockSpec((None, None, d, EXPERT_FF), lambda e, i: (layer, e, 0, 0)),
                  pl.BlockSpec((None, None, EXPERT_FF, d), lambda e, i: (layer, e, 0, 0))],
        out_specs=pl.BlockSpec((None, tm, d), lambda e, i: (e, i, 0)),
        out_shape=jax.ShapeDtypeStruct((N_EXPERTS, rows, d), BF16),
        compiler_params=_cparams("parallel", "arbitrary"),
        name="moe_ffn",
    )(xc.reshape(N_EXPERTS, rows, d), w_gate, w_up, w_down)
    out = pl.pallas_call(
        _moe_scatter_kernel,
        grid=(nsub,),
        in_specs=[pl.BlockSpec((N_EXPERTS, None, MOE_CAP, d), lambda i: (0, i, 0, 0)),
                  pl.BlockSpec((MOE_SUB, LANE), lambda i: (i, 0)),
                  pl.BlockSpec((MOE_SUB, d), lambda i: (i, 0))],
        out_specs=pl.BlockSpec((MOE_SUB, d), lambda i: (i, 0)),
        out_shape=jax.ShapeDtypeStruct((t, d), F32),
        compiler_params=_cparams("parallel"),
        name="moe_scatter",
    )(y.reshape(N_EXPERTS, nsub, MOE_CAP, d), meta, h)
    return out, jnp.max(flag)


def moe_layer(h, g, w_router, b_router, w_gate, w_up, w_down, layer):
    out, overflow = moe_routed(h, g, w_router, b_router, w_gate, w_up, w_down, layer)

    def dense():
        w_gu = jnp.concatenate([w_gate[layer], w_up[layer]], axis=-1).astype(BF16)
        return moe_residual(h, g, w_router, b_router, w_gu, w_down[layer].astype(BF16))

    return lax.cond(overflow > 0.0, dense, lambda: out)


def _dsa_prep_kernel(x_ref, ca_ref, sa_ref, ci_ref, slo_ref, shi_ref,
                     q_ref, k_ref, v_ref, iq_ref, ik_ref, iw_ref):
    ca, sa = ca_ref[...], sa_ref[...]
    ci, slo, shi = ci_ref[...], slo_ref[...], shi_ref[...]
    scale = A_HEAD_DIM ** -0.5 * math.log2(math.e)
    for h in range(A_HEADS):
        xh = x_ref[:, h * LANE:(h + 1) * LANE]
        q_ref[:, h * LANE:(h + 1) * LANE] = (_rope128(xh, ca, sa) * scale).astype(BF16)
    off = A_HEADS * A_HEAD_DIM
    for h in range(A_KV_HEADS):
        xh = x_ref[:, off + h * LANE:off + (h + 1) * LANE]
        k_ref[:, h * LANE:(h + 1) * LANE] = _rope128(xh, ca, sa).astype(BF16)
    off += A_KV_HEADS * A_HEAD_DIM
    v_ref[...] = x_ref[:, off:off + A_KV_HEADS * A_HEAD_DIM].astype(BF16)
    off += A_KV_HEADS * A_HEAD_DIM
    for h in range(IDX_HEADS * IDX_DIM // LANE):
        xh = x_ref[:, off + h * LANE:off + (h + 1) * LANE]
        iq_ref[:, h * LANE:(h + 1) * LANE] = _rope64(xh, ci, slo, shi).astype(BF16)
    off += IDX_HEADS * IDX_DIM
    misc = x_ref[:, off:off + LANE]
    ik_ref[...] = _rope64(misc, ci, slo, shi)[:, :IDX_DIM].astype(BF16)
    iw_ref[...] = misc[:, IDX_DIM:IDX_DIM + IDX_HEADS] * IDX_SCALE


def dsa_prep(proj, tabs, *, tm=512):
    t, n = proj.shape
    tm = min(tm, t)
    row = lambda w: pl.BlockSpec((tm, w), lambda i: (i, 0))
    return pl.pallas_call(
        _dsa_prep_kernel,
        grid=(t // tm,),
        in_specs=[row(n)] + [row(LANE)] * 5,
        out_specs=[row(1024), row(256), row(256), row(512), row(IDX_DIM), row(IDX_HEADS)],
        out_shape=[jax.ShapeDtypeStruct((t, 1024), BF16), jax.ShapeDtypeStruct((t, 256), BF16),
                   jax.ShapeDtypeStruct((t, 256), BF16), jax.ShapeDtypeStruct((t, 512), BF16),
                   jax.ShapeDtypeStruct((t, IDX_DIM), BF16), jax.ShapeDtypeStruct((t, IDX_HEADS), F32)],
        compiler_params=_cparams("parallel"),
        name="dsa_prep",
    )(proj, *tabs)


def _dsa_kernel(q_ref, k_ref, v_ref, iq_ref, ik_ref, iw_ref, o_ref,
                key_ref, m_ref, l_ref, acc_ref, *, tk, topk):
    qi = pl.program_id(1)
    nq = q_ref.shape[0]
    rep = A_HEADS // A_KV_HEADS
    n_tiles = ((qi + 1) * nq + tk - 1) // tk
    row = lax.broadcasted_iota(I32, (nq, tk), 0)
    col = lax.broadcasted_iota(I32, (nq, tk), 1)
    q_chunk = (qi * nq + row) // CHUNK

    def admissible(off):
        return ((off + col) // CHUNK) <= q_chunk

    iq = iq_ref[...]
    iw = iw_ref[...]

    def score_body(t, carry):
        off = pl.multiple_of(t * tk, tk)
        ikt = ik_ref[pl.ds(off, tk), :]
        s = jnp.zeros((nq, tk), F32)
        for h in range(IDX_HEADS):
            rel = _dot_nt(iq[:, h * IDX_DIM:(h + 1) * IDX_DIM], ikt)
            s = s + iw[:, h:h + 1] * jnp.maximum(rel, 0.0)
        key_ref[:, pl.ds(off, tk)] = jnp.where(admissible(off), s, NEG_INF)
        return carry

    lax.fori_loop(0, n_tiles, score_body, 0)

    def scan(fn, init):
        def body(t, acc):
            off = pl.multiple_of(t * tk, tk)
            blk = key_ref[:, pl.ds(off, tk)]
            for c in range(tk // LANE):
                acc = fn(blk[:, c * LANE:(c + 1) * LANE], acc)
            return acc
        return lax.fori_loop(0, n_tiles, body, init)

    def rowsum(x):
        return jnp.sum(x, axis=1, keepdims=True)

    def count_ge(v):
        return rowsum(scan(lambda blk, acc: acc + jnp.where(blk >= v, 1.0, 0.0), jnp.zeros((nq, LANE), F32)))

    big = jnp.full((nq, LANE), -NEG_INF, F32)
    zeros = jnp.zeros((nq, LANE), F32)
    lo_p, hi_p, n_p = scan(
        lambda blk, acc: (jnp.minimum(acc[0], jnp.where(blk > NEG_INF, blk, big)), jnp.maximum(acc[1], blk),
                          acc[2] + jnp.where(blk > NEG_INF, 1.0, 0.0)), (big, -big, zeros))
    enough = rowsum(n_p) >= topk
    lo0 = jnp.where(enough, jnp.min(lo_p, axis=1, keepdims=True), NEG_INF)
    hi0 = jnp.where(enough, jnp.max(hi_p, axis=1, keepdims=True), NEG_INF)

    def bisect_body(i, bracket):
        lo, hi = bracket
        mid = lo + 0.5 * (hi - lo)
        ge = count_ge(mid) >= topk
        return jnp.where(ge, mid, lo), jnp.where(ge, hi, mid)

    _, hi = lax.fori_loop(0, BISECT_STEPS, bisect_body, (lo0, hi0))

    def walk_cond(state):
        return state[1] > 0

    def walk_body(state):
        v, _ = state
        zero = jnp.zeros((nq, LANE), F32)
        cnt_p, nxt_p = scan(lambda blk, acc: (acc[0] + jnp.where(blk >= v, 1.0, 0.0),
                                              jnp.maximum(acc[1], jnp.where(blk < v, blk, -big))),
                            (zero, -big))
        found = rowsum(cnt_p) >= topk
        v = jnp.where(found, v, jnp.max(nxt_p, axis=1, keepdims=True))
        return v, jnp.sum(jnp.where(found, 0, 1))

    hi_val = jnp.max(scan(lambda blk, acc: jnp.maximum(acc, jnp.where(blk <= hi, blk, -big)), -big),
                     axis=1, keepdims=True)
    thr, _ = lax.while_loop(walk_cond, walk_body, (hi_val, jnp.int32(1)))
    n_gt = rowsum(scan(lambda blk, acc: acc + jnp.where(blk > thr, 1.0, 0.0), jnp.zeros((nq, LANE), F32)))
    need = topk - n_gt

    m_ref[...] = jnp.full(m_ref.shape, NEG_INF, F32)
    l_ref[...] = jnp.zeros(l_ref.shape, F32)
    acc_ref[...] = jnp.zeros(acc_ref.shape, F32)
    upper = (lax.broadcasted_iota(I32, (LANE, LANE), 0)
             < lax.broadcasted_iota(I32, (LANE, LANE), 1)).astype(BF16)
    qg = [jnp.concatenate([q_ref[:, (g * rep + r) * LANE:(g * rep + r + 1) * LANE]
                           for r in range(rep)], axis=0) for g in range(A_KV_HEADS)]

    def attn_body(t, run):
        off = pl.multiple_of(t * tk, tk)
        keyt = key_ref[:, pl.ds(off, tk)]
        adm = admissible(off)
        sels = []
        for c in range(tk // LANE):
            kc = keyt[:, c * LANE:(c + 1) * LANE]
            eqf = jnp.where(kc == thr, 1.0, 0.0)
            before = _dot(eqf.astype(BF16), upper) + run
            take = jnp.where(kc > thr, 1.0, jnp.where(before < need, eqf, 0.0))
            sels.append(take)
            run = run + jnp.sum(eqf, axis=1, keepdims=True)
        sel = jnp.where(adm, jnp.concatenate(sels, axis=1), 0.0) > 0.5
        sel_r = jnp.concatenate([sel] * rep, axis=0)
        groups = range(A_KV_HEADS)
        ss = [jnp.where(sel_r, _dot_nt(qg[g], k_ref[pl.ds(off, tk), g * LANE:(g + 1) * LANE]), NEG_INF)
              for g in groups]
        m_old = [m_ref[g] for g in groups]
        m_new = [jnp.maximum(m_old[g], _rowmax(ss[g])) for g in groups]
        ps = [jnp.exp2(ss[g] - m_new[g]) for g in groups]
        pv = [_dot(ps[g].astype(BF16), v_ref[pl.ds(off, tk), g * LANE:(g + 1) * LANE]) for g in groups]
        for g in groups:
            alpha = jnp.exp2(m_old[g] - m_new[g])
            l_ref[g] = alpha * l_ref[g] + _rowsum(ps[g])
            acc_ref[g] = alpha * acc_ref[g] + pv[g]
            m_ref[g] = m_new[g]
        return run

    lax.fori_loop(0, n_tiles, attn_body, jnp.zeros((nq, 1), F32))
    for g in range(A_KV_HEADS):
        out = acc_ref[g] / l_ref[g]
        for r in range(rep):
            h = g * rep + r
            o_ref[:, h * LANE:(h + 1) * LANE] = out[r * nq:(r + 1) * nq].astype(o_ref.dtype)


def dsa_attention(q, k, v, iq, ik, iw, *, nq=128, tk=512):
    b, s, _ = q.shape
    tk = min(tk, s)
    topk = min(TOPK_MAX, s // 4)
    rep = A_HEADS // A_KV_HEADS
    qspec = lambda w: pl.BlockSpec((None, nq, w), lambda bi, qi: (bi, qi, 0))
    kspec = lambda w: pl.BlockSpec((None, s, w), lambda bi, qi: (bi, 0, 0))
    return pl.pallas_call(
        functools.partial(_dsa_kernel, tk=tk, topk=topk),
        grid=(b, s // nq),
        in_specs=[qspec(1024), kspec(256), kspec(256), qspec(512), kspec(IDX_DIM), qspec(IDX_HEADS)],
        out_specs=qspec(1024),
        out_shape=jax.ShapeDtypeStruct((b, s, 1024), BF16),
        scratch_shapes=[pltpu.VMEM((nq, s), F32),
                        pltpu.VMEM((A_KV_HEADS, rep * nq, 1), F32),
                        pltpu.VMEM((A_KV_HEADS, rep * nq, 1), F32),
                        pltpu.VMEM((A_KV_HEADS, rep * nq, LANE), F32)],
        compiler_params=_cparams("parallel", "arbitrary"),
        name="dsa_attention",
    )(q, k, v, iq, ik, iw)


def _causal_conv(x, tail_ref, xp_ref, w):
    r = x.shape[0]
    width = w.shape[0]
    xp_ref[0:8, :] = tail_ref[...]
    xp_ref[8:8 + r, :] = x
    tail_ref[...] = x[r - 8:r, :]
    acc = x * w[width - 1:width, :]
    for j in range(width - 1):
        acc = acc + xp_ref[pl.ds(8 - (width - 1) + j, r), :] * w[j:j + 1, :]
    return acc


def _unit_lower_inverse(mats):
    n = mats[0].shape[0]
    eye = (lax.broadcasted_iota(I32, (n, n), 0) == lax.broadcasted_iota(I32, (n, n), 1)).astype(F32)
    ps = [-a for a in mats]
    ts = [eye + p for p in ps]
    for _ in range(int(math.ceil(math.log2(n))) - 1):
        ps = [_bdot(p, p) for p in ps]
        ts = [t + _bdot(t, p) for t, p in zip(ts, ps)]
    return ts


def _gdn_kernel(qkv_ref, z_ref, misc_ref, convw_ref, alog_ref, dtb_ref, nw_ref, o_ref,
                tail_ref, xp_ref, q_s, k_s, v_s, beta_s, g_s, state_ref):
    rows = qkv_ref.shape[0]
    hd = B_HEAD_DIM
    nh = B_HEADS

    @pl.when(pl.program_id(1) == 0)
    def _():
        tail_ref[...] = jnp.zeros(tail_ref.shape, F32)
        state_ref[...] = jnp.zeros(state_ref.shape, F32)

    y = _silu(_causal_conv(qkv_ref[...], tail_ref, xp_ref, convw_ref[...]))
    for h in range(nh):
        qh = y[:, h * hd:(h + 1) * hd]
        kh = y[:, (nh + h) * hd:(nh + h + 1) * hd]
        q_s[:, h * hd:(h + 1) * hd] = qh * lax.rsqrt(jnp.sum(qh * qh, -1, keepdims=True) + 1e-6) * hd ** -0.5
        k_s[:, h * hd:(h + 1) * hd] = kh * lax.rsqrt(jnp.sum(kh * kh, -1, keepdims=True) + 1e-6)
    v_s[...] = y[:, 2 * nh * hd:]
    misc = misc_ref[...]
    beta_s[...] = _sigmoid(misc)
    g_s[...] = -jnp.exp(alog_ref[...]) * _softplus(misc + dtb_ref[...])

    ri = lax.broadcasted_iota(I32, (CHUNK, CHUNK), 0)
    ci = lax.broadcasted_iota(I32, (CHUNK, CHUNK), 1)
    incl = ri >= ci
    strict = ri > ci
    lower01 = incl.astype(BF16)
    nw = nw_ref[...]

    def chunk_body(c, carry):
        r0 = pl.multiple_of(c * CHUNK, CHUNK)
        rs = pl.ds(r0, CHUNK)
        gc = _dot01_left(lower01, g_s[rs, :])
        gct = gc.T
        beta = beta_s[rs, :]
        eg = jnp.exp(gc)
        g_last = gc[CHUNK - 1:CHUNK, :]
        e_last = jnp.exp(g_last)
        e_rem = jnp.exp(g_last - gc)
        heads = range(nh)
        hs = [slice(h * hd, (h + 1) * hd) for h in heads]
        col = lambda x, h: x[:, nh + h:nh + h + 1]
        qh = [q_s[rs, hs[h]] for h in heads]
        kh = [k_s[rs, hs[h]] for h in heads]
        bcol = [beta[:, h:h + 1] for h in heads]
        kb = [kh[h] * bcol[h] for h in heads]
        kk = [_bdot_nt(kb[h], kh[h]) for h in heads]
        qk = [_bdot_nt(qh[h], kh[h]) for h in heads]
        st = [state_ref[h] for h in heads]
        q_st = [_bdot(qh[h] * col(eg, h), st[h]) for h in heads]
        decay = []
        for h in heads:
            diff = col(gc, h) - gct[nh + h:nh + h + 1, :]
            decay.append(jnp.where(incl, jnp.exp(jnp.where(incl, diff, 0.0)), 0.0))
        t_mat = _unit_lower_inverse([jnp.where(strict, kk[h] * decay[h], 0.0) for h in heads])
        uw = [_bdot(t_mat[h], jnp.concatenate([v_s[rs, hs[h]] * bcol[h], kb[h] * col(eg, h)], axis=1))
              for h in heads]
        w_st = [_bdot(uw[h][:, hd:], st[h]) for h in heads]
        v_new = [uw[h][:, :hd] - w_st[h] for h in heads]
        o = [q_st[h] + _bdot(qk[h] * decay[h], v_new[h]) for h in heads]
        upd = [_bdot((kh[h] * col(e_rem, h)).T, v_new[h]) for h in heads]
        for h in heads:
            state_ref[h] = st[h] * col(e_last, h) + upd[h]
            on = o[h] * lax.rsqrt(jnp.mean(o[h] * o[h], -1, keepdims=True) + NORM_EPS) * nw
            o_ref[rs, hs[h]] = (on * _silu(z_ref[rs, hs[h]])).astype(o_ref.dtype)
        return carry

    lax.fori_loop(0, rows // CHUNK, chunk_body, 0)


def gdn_mixer(proj, conv_w, alog_row, dtb_row, norm_w, *, batch, rows=256):
    t = proj.shape[0]
    s = t // batch
    rows = min(rows, s)
    nr = s // rows
    nqkv = 3 * B_HEADS * B_HEAD_DIM
    nz = B_HEADS * B_HEAD_DIM
    const = lambda shape: pl.BlockSpec(shape, lambda b, r: (0, 0))
    return pl.pallas_call(
        _gdn_kernel,
        grid=(batch, nr),
        in_specs=[pl.BlockSpec((rows, nqkv), lambda b, r: (b * nr + r, 0)),
                  pl.BlockSpec((rows, nz), lambda b, r: (b * nr + r, nqkv // nz)),
                  pl.BlockSpec((rows, LANE), lambda b, r: (b * nr + r, (nqkv + nz) // LANE)),
                  const((CONV_WIDTH, nqkv)), const((1, LANE)), const((1, LANE)), const((1, LANE))],
        out_specs=pl.BlockSpec((rows, nz), lambda b, r: (b * nr + r, 0)),
        out_shape=jax.ShapeDtypeStruct((t, nz), BF16),
        scratch_shapes=[pltpu.VMEM((8, nqkv), F32), pltpu.VMEM((rows + 8, nqkv), F32),
                        pltpu.VMEM((rows, nz), F32), pltpu.VMEM((rows, nz), F32), pltpu.VMEM((rows, nz), F32),
                        pltpu.VMEM((rows, LANE), F32), pltpu.VMEM((rows, LANE), F32),
                        pltpu.VMEM((B_HEADS, B_HEAD_DIM, B_HEAD_DIM), F32)],
        compiler_params=_cparams("parallel", "arbitrary"),
        name="gdn",
    )(proj, proj, proj, conv_w, alog_row, dtb_row, norm_w)


def _mla_prep_kernel(q_ref, kr_ref, ci_ref, slo_ref, shi_ref, qo_ref, kro_ref):
    ci, slo, shi = ci_ref[...], slo_ref[...], shi_ref[...]
    scale = (C_NOPE + C_ROPE) ** -0.5 * math.log2(math.e)
    for h in range(C_HEADS):
        base = h * 2 * LANE
        qo_ref[:, base:base + LANE] = (q_ref[:, base:base + LANE] * scale).astype(BF16)
        qo_ref[:, base + LANE:base + 2 * LANE] = (
            _rope64(q_ref[:, base + LANE:base + 2 * LANE], ci, slo, shi) * scale).astype(BF16)
    kro_ref[...] = _rope64(kr_ref[...], ci, slo, shi).astype(BF16)


def mla_prep(q_raw, proj_c, tabs, *, tm=512):
    t, n = q_raw.shape
    tm = min(tm, t)
    row = lambda w: pl.BlockSpec((tm, w), lambda i: (i, 0))
    return pl.pallas_call(
        _mla_prep_kernel,
        grid=(t // tm,),
        in_specs=[row(n), pl.BlockSpec((tm, LANE), lambda i: (i, (C_Q_RANK + C_KV_RANK) // LANE)),
                  row(LANE), row(LANE), row(LANE)],
        out_specs=[row(n), row(LANE)],
        out_shape=[jax.ShapeDtypeStruct((t, n), BF16), jax.ShapeDtypeStruct((t, LANE), BF16)],
        compiler_params=_cparams("parallel"),
        name="mla_prep",
    )(q_raw, proj_c, *tabs)


def _mla_kernel(q_ref, kv_ref, kr_ref, o_ref, m_ref, l_ref, acc_ref, *, hb):
    qi = pl.program_id(2)
    tq = q_ref.shape[0]
    heads = range(hb)
    row = lax.broadcasted_iota(I32, (tq, tq), 0)
    col = lax.broadcasted_iota(I32, (tq, tq), 1)
    diag_mask = (col // CHUNK) <= (row // CHUNK)
    m_ref[...] = jnp.full(m_ref.shape, NEG_INF, F32)
    l_ref[...] = jnp.zeros(l_ref.shape, F32)
    acc_ref[...] = jnp.zeros(acc_ref.shape, F32)
    qs = [q_ref[:, h * 2 * LANE:(h + 1) * 2 * LANE] for h in heads]

    def step(off, on_diagonal):
        rows = pl.ds(off, tq)
        kr = kr_ref[rows, :]
        ss = [_dot_nt(qs[h], jnp.concatenate([kv_ref[rows, h * 2 * LANE:h * 2 * LANE + LANE], kr], axis=1))
              for h in heads]
        if on_diagonal:
            ss = [jnp.where(diag_mask, s, NEG_INF) for s in ss]
        m_old = [m_ref[h] for h in heads]
        m_new = [jnp.maximum(m_old[h], _rowmax(ss[h])) for h in heads]
        ps = [jnp.exp2(ss[h] - m_new[h]) for h in heads]
        pv = [_dot(ps[h].astype(BF16), kv_ref[rows, h * 2 * LANE + LANE:(h + 1) * 2 * LANE]) for h in heads]
        for h in heads:
            alpha = jnp.exp2(m_old[h] - m_new[h])
            l_ref[h] = alpha * l_ref[h] + _rowsum(ps[h])
            acc_ref[h] = alpha * acc_ref[h] + pv[h]
            m_ref[h] = m_new[h]

    def body(t, carry):
        step(pl.multiple_of(t * tq, tq), False)
        return carry

    lax.fori_loop(0, qi, body, 0)
    step(pl.multiple_of(qi * tq, tq), True)
    for h in heads:
        o_ref[:, h * LANE:(h + 1) * LANE] = (acc_ref[h] / l_ref[h]).astype(o_ref.dtype)


def mla_attention(q, kv, kr, *, tq=512, hb=4):
    b, s, _ = q.shape
    tq = min(tq, s)
    return pl.pallas_call(
        functools.partial(_mla_kernel, hb=hb),
        grid=(b, C_HEADS // hb, s // tq),
        in_specs=[pl.BlockSpec((None, tq, hb * 2 * LANE), lambda bi, h, qi: (bi, qi, h)),
                  pl.BlockSpec((None, s, hb * 2 * LANE), lambda bi, h, qi: (bi, 0, h)),
                  pl.BlockSpec((None, s, LANE), lambda bi, h, qi: (bi, 0, 0))],
        out_specs=pl.BlockSpec((None, tq, hb * LANE), lambda bi, h, qi: (bi, qi, h)),
        out_shape=jax.ShapeDtypeStruct((b, s, C_HEADS * C_V), BF16),
        scratch_shapes=[pltpu.VMEM((hb, tq, 1), F32), pltpu.VMEM((hb, tq, 1), F32),
                        pltpu.VMEM((hb, tq, LANE), F32)],
        compiler_params=_cparams("parallel", "parallel", "arbitrary"),
        name="mla_attention",
    )(q, kv, kr)


def _ssd_kernel(zx_ref, misc_ref, convw_ref, convb_ref, dtb_ref, a_ref, aexp_ref, dskip_ref,
                nw_ref, expand_ref, o_ref,
                tail_ref, xp_ref, xs_s, xdt_s, bm_s, cm_s, la_s, laexp_s, y_s, state_ref):
    rows = zx_ref.shape[0]
    pd = D_HEAD_DIM
    gw = D_INNER // D_GROUPS
    pairs_per_group = gw // LANE

    @pl.when(pl.program_id(1) == 0)
    def _():
        tail_ref[...] = jnp.zeros(tail_ref.shape, F32)
        state_ref[...] = jnp.zeros(state_ref.shape, F32)

    y = _silu(_causal_conv(zx_ref[:, D_INNER:], tail_ref, xp_ref, convw_ref[...]) + convb_ref[...])
    xs = y[:, :D_INNER]
    xs_s[...] = xs
    bm_s[...] = y[:, D_INNER:D_INNER + D_GROUPS * D_STATE]
    cm_s[...] = y[:, D_INNER + D_GROUPS * D_STATE:]
    dt = _softplus(misc_ref[...] + dtb_ref[...])
    dt_exp = _dot01_right(dt, expand_ref[...])
    xdt_s[...] = xs * dt_exp
    la_s[...] = dt * a_ref[...]
    laexp_s[...] = dt_exp * aexp_ref[...]

    ri = lax.broadcasted_iota(I32, (CHUNK, CHUNK), 0)
    ci = lax.broadcasted_iota(I32, (CHUNK, CHUNK), 1)
    incl = ri >= ci
    lower01 = incl.astype(BF16)
    lane_lo = lax.broadcasted_iota(I32, (CHUNK, LANE), 1) < pd

    def chunk_body(c, carry):
        r0 = pl.multiple_of(c * CHUNK, CHUNK)
        rs = pl.ds(r0, CHUNK)
        cum = _dot01_left(lower01, la_s[rs, :])
        cumt = cum.T
        cum_e = _dot01_left(lower01, laexp_s[rs, :])
        ecum = jnp.exp(cum_e)
        last = cum_e[CHUNK - 1:CHUNK, :]
        dec_st = jnp.exp(last - cum_e)
        cdec = jnp.exp(last)
        for g in range(D_GROUPS):
            bm = bm_s[rs, g * D_STATE:(g + 1) * D_STATE]
            cm = cm_s[rs, g * D_STATE:(g + 1) * D_STATE]
            cb = _bdot_nt(cm, bm)
            bmt = bm.T.astype(BF16)
            gs = slice(g * gw, (g + 1) * gw)
            y_off = _bdot(cm, state_ref[:, gs]) * ecum[:, gs]
            for pp in range(pairs_per_group):
                pidx = g * pairs_per_group + pp
                ps = slice(pidx * LANE, (pidx + 1) * LANE)
                xp = xdt_s[rs, ps]
                ys = []
                for hh in range(2):
                    h = 2 * pidx + hh
                    diff = cum[:, h:h + 1] - cumt[h:h + 1, :]
                    lmat = jnp.where(incl, jnp.exp(jnp.where(incl, diff, 0.0)), 0.0)
                    ys.append(_bdot(cb * lmat, xp))
                y_diag = jnp.where(lane_lo, ys[0], ys[1])
                y_s[rs, ps] = (y_diag + y_off[:, pp * LANE:(pp + 1) * LANE]
                               + dskip_ref[:, ps] * xs_s[rs, ps])
                s_new = _dot(bmt, (xp * dec_st[:, ps]).astype(BF16))
                state_ref[:, ps] = state_ref[:, ps] * cdec[:, ps] + s_new
        return carry

    lax.fori_loop(0, rows // CHUNK, chunk_body, 0)
    yv = y_s[...] * _silu(zx_ref[:, :D_INNER])
    for g in range(D_GROUPS):
        gs = slice(g * gw, (g + 1) * gw)
        yg = yv[:, gs]
        ms = jnp.mean(yg * yg, axis=-1, keepdims=True)
        o_ref[:, gs] = (yg * lax.rsqrt(ms + NORM_EPS) * nw_ref[:, gs]).astype(o_ref.dtype)


def ssd_mixer(proj, conv_w, conv_b, dtb_row, a_row, a_exp, dskip_exp, norm_w, expand, *, batch, rows=256):
    t = proj.shape[0]
    s = t // batch
    rows = min(rows, s)
    nr = s // rows
    nx = D_INNER + 2 * D_GROUPS * D_STATE
    const = lambda shape: pl.BlockSpec(shape, lambda b, r: (0, 0))
    return pl.pallas_call(
        _ssd_kernel,
        grid=(batch, nr),
        in_specs=[pl.BlockSpec((rows, D_INNER + nx), lambda b, r: (b * nr + r, 0)),
                  pl.BlockSpec((rows, LANE), lambda b, r: (b * nr + r, (D_INNER + nx) // LANE)),
                  const((CONV_WIDTH, nx)), const((1, nx)), const((1, LANE)), const((1, LANE)),
                  const((1, D_INNER)), const((1, D_INNER)), const((1, D_INNER)), const((LANE, D_INNER))],
        out_specs=pl.BlockSpec((rows, D_INNER), lambda b, r: (b * nr + r, 0)),
        out_shape=jax.ShapeDtypeStruct((t, D_INNER), BF16),
        scratch_shapes=[pltpu.VMEM((8, nx), F32), pltpu.VMEM((rows + 8, nx), F32),
                        pltpu.VMEM((rows, D_INNER), F32), pltpu.VMEM((rows, D_INNER), F32),
                        pltpu.VMEM((rows, D_GROUPS * D_STATE), F32), pltpu.VMEM((rows, D_GROUPS * D_STATE), F32),
                        pltpu.VMEM((rows, LANE), F32), pltpu.VMEM((rows, D_INNER), F32),
                        pltpu.VMEM((rows, D_INNER), F32),
                        pltpu.VMEM((D_STATE, D_INNER), F32)],
        compiler_params=_cparams("parallel", "arbitrary"),
        name="ssd",
    )(proj, proj, conv_w, conv_b, dtb_row, a_row, a_exp, dskip_exp, norm_w, expand)


def _rope_tables(positions):
    pos = positions.reshape(-1).astype(F32)[:, None]

    def cs(dim):
        inv = jnp.power(ROPE_THETA, -jnp.arange(0, dim, 2, dtype=F32) / dim)
        ang = pos * inv
        return jnp.cos(ang), jnp.sin(ang)

    c, s = cs(A_HEAD_DIM)
    tab_a = (jnp.concatenate([c, c], -1), jnp.concatenate([-s, s], -1))
    c, s = cs(IDX_DIM)
    z = jnp.zeros_like(s)
    tab_i = (jnp.concatenate([c, c, c, c], -1), jnp.concatenate([-s, z, -s, z], -1),
             jnp.concatenate([z, s, z, s], -1))
    return tab_a, tab_i


def _pad_cols(w, n):
    return jnp.pad(w, ((0, 0), (0, n - w.shape[1])))


def _lane_row(v, start):
    return jnp.zeros((1, LANE), F32).at[0, start:start + v.shape[0]].set(v.astype(F32))


def _moe_weights(i, w_group, b_group, w_expert, b_expert, w_gate, w_up, w_down):
    w_router = _pad_cols(jnp.concatenate([w_group[i], w_expert[i]], axis=1), LANE).astype(F32)
    b_router = _pad_cols(jnp.concatenate([b_group[i], b_expert[i]])[None], LANE).astype(F32)
    return w_router, b_router, w_gate, w_up, w_down, i


def kernel(x, p, positions, norm_mix, norm_ffn, norm_ple, norm_final, ev_w_in, ev_w_out, gdn_conv_w,
           gdn_a_log, gdn_dt_bias, gdn_norm, od_w_in, od_w_out, mla_q_norm, mla_kv_norm, mla_w_uq,
           mla_w_ukv, ssm_conv_w, ssm_conv_b, ssm_a_log, ssm_dt_bias, ssm_d_skip, ssm_norm, moe_w_group,
           moe_b_group, moe_w_expert, moe_b_expert, moe_w_gate, moe_w_up, moe_w_down, ple_w_proj,
           ple_w_gate):
    b, s, d = x.shape
    t = b * s
    tab_a, tab_i = _rope_tables(positions)
    h = x.reshape(t, d)
    moe_args = (moe_w_group, moe_b_group, moe_w_expert, moe_b_expert, moe_w_gate, moe_w_up, moe_w_down)

    w = ev_w_in[0]
    o = np_cumsum((1024, 256, 256, 512, 64, 8, 3072, 1024, 8, 8))
    w_a = _pad_cols(w[:, :o[6]], 2176).astype(BF16)
    w_b = _pad_cols(jnp.concatenate([w[:, o[6]:o[8]], w[:, o[8]:]], axis=1), 4224).astype(BF16)
    proj_a = norm_matmul(h, norm_mix[0], w_a, k=d)
    proj_b = norm_matmul(h, norm_mix[0], w_b, k=d, tn=1408)
    q, k, v, iq, ik, iw = dsa_prep(proj_a, tab_a + tab_i)
    r3 = lambda a: a.reshape(b, s, a.shape[-1])
    o_a = dsa_attention(r3(q), r3(k), r3(v), r3(iq), r3(ik), r3(iw)).reshape(t, -1)
    o_b = gdn_mixer(proj_b, gdn_conv_w[0].astype(F32), _lane_row(gdn_a_log[0], B_HEADS),
                    _lane_row(gdn_dt_bias[0], B_HEADS), gdn_norm[0].reshape(1, -1).astype(F32), batch=b)
    h = outproj_residual(o_a, o_b, ev_w_out[0].astype(BF16), h)
    h = moe_layer(h, norm_ffn[0], *_moe_weights(0, *moe_args))
    h = ple_residual(h, norm_ple[0], ple_w_gate[0].astype(BF16), p[0].reshape(t, -1),
                     ple_w_proj[0].astype(BF16))

    w = od_w_in[0]
    o = np_cumsum((512, 512, 64, 1024, 1536, 16))
    w_c = _pad_cols(w[:, :o[3]], 1152).astype(BF16)
    w_d = _pad_cols(w[:, o[3]:], 2688).astype(BF16)
    proj_c = norm_matmul(h, norm_mix[1], w_c, k=d)
    proj_d = norm_matmul(h, norm_mix[1], w_d, k=d, tn=896)
    w_uq = jnp.pad(mla_w_uq[0].reshape(C_Q_RANK, C_HEADS, C_NOPE + C_ROPE),
                   ((0, 0), (0, 0), (0, 2 * LANE - C_NOPE - C_ROPE))).reshape(C_Q_RANK, -1).astype(BF16)
    q_raw = norm_matmul(proj_c, mla_q_norm[0], w_uq, k=C_Q_RANK, xblk=0, tn=1024)
    kv = norm_matmul(proj_c, mla_kv_norm[0], mla_w_ukv[0].astype(BF16), k=C_KV_RANK, xblk=1, tn=1024,
                     out_dtype=BF16)
    q, kr = mla_prep(q_raw, proj_c, tab_i)
    o_c = mla_attention(r3(q), r3(kv), r3(kr)).reshape(t, -1)
    a_heads = -jnp.exp(ssm_a_log[0].astype(F32))
    expand = (jnp.arange(LANE)[:, None] == (jnp.arange(D_INNER)[None, :] // D_HEAD_DIM)).astype(BF16)
    o_d = ssd_mixer(proj_d, ssm_conv_w[0].astype(F32), ssm_conv_b[0].reshape(1, -1).astype(F32),
                    _lane_row(ssm_dt_bias[0], 0), _lane_row(a_heads, 0),
                    jnp.repeat(a_heads, D_HEAD_DIM)[None], jnp.repeat(ssm_d_skip[0].astype(F32), D_HEAD_DIM)[None],
                    ssm_norm[0].reshape(1, -1).astype(F32), expand, batch=b)
    h = outproj_residual(o_c, o_d, od_w_out[0].astype(BF16), h)
    h = moe_layer(h, norm_ffn[1], *_moe_weights(1, *moe_args))
    h = ple_residual(h, norm_ple[1], ple_w_gate[1].astype(BF16), p[1].reshape(t, -1),
                     ple_w_proj[1].astype(BF16))
    return rmsnorm(h, norm_final).reshape(b, s, d)


def np_cumsum(sizes):
    out, acc = [], 0
    for v in sizes:
        out.append(acc)
        acc += v
    return out
```

```python
import functools
import math

import jax
import jax.numpy as jnp
from jax import lax
from jax.experimental import pallas as pl
from jax.experimental.pallas import tpu as pltpu

F32 = jnp.float32
BF16 = jnp.bfloat16
I32 = jnp.int32

CHUNK = 64
ROPE_THETA = 10000.0
NORM_EPS = 1e-6
NEG_INF = -1e30
BISECT_STEPS = 16

A_HEADS, A_KV_HEADS, A_HEAD_DIM = 8, 2, 128
IDX_HEADS, IDX_DIM = 8, 64
IDX_SCALE = (IDX_HEADS * IDX_DIM) ** -0.5
TOPK_MAX = 256
B_HEADS, B_HEAD_DIM, CONV_WIDTH = 8, 128, 4
C_HEADS, C_Q_RANK, C_KV_RANK, C_NOPE, C_ROPE, C_V = 8, 512, 512, 128, 64, 128
D_INNER, D_HEAD_DIM, D_GROUPS, D_STATE = 1024, 64, 2, 128
D_HEADS = D_INNER // D_HEAD_DIM
N_GROUPS, EXPERTS_PER_GROUP, EXPERT_FF = 4, 8, 256
N_EXPERTS = N_GROUPS * EXPERTS_PER_GROUP

LANE = 128
VMEM_BYTES_V7X = 64 * 1024 * 1024
VMEM_LIMIT = VMEM_BYTES_V7X - 8 * 1024 * 1024


def _cparams(*sem):
    return pltpu.CompilerParams(dimension_semantics=sem, vmem_limit_bytes=VMEM_LIMIT)


def _dot(a, b):
    return jnp.dot(a, b, preferred_element_type=F32)


def _dot_nt(a, b):
    return lax.dot_general(a, b, (((1,), (1,)), ((), ())), preferred_element_type=F32)


def _bdot(a, b):
    return _dot(a.astype(BF16), b.astype(BF16))


def _bdot_nt(a, b):
    return _dot_nt(a.astype(BF16), b.astype(BF16))


def _split3(x):
    hi = x.astype(BF16)
    r1 = x - hi.astype(F32)
    mid = r1.astype(BF16)
    lo = (r1 - mid.astype(F32)).astype(BF16)
    return hi, mid, lo


def _dot01_left(m01, x):
    hi, mid, lo = _split3(x)
    return _dot(m01, hi) + _dot(m01, mid) + _dot(m01, lo)


def _dot01_right(x, m01):
    hi, mid, lo = _split3(x)
    return _dot(hi, m01) + _dot(mid, m01) + _dot(lo, m01)


def _sigmoid(x):
    return 1.0 / (1.0 + jnp.exp(-x))


def _silu(x):
    return x * _sigmoid(x)


def _softplus(x):
    return jnp.maximum(x, 0.0) + jnp.log1p(jnp.exp(-jnp.abs(x)))


def _lane_fold(x, op):
    blocks = [x[:, c * LANE:(c + 1) * LANE] for c in range(x.shape[1] // LANE)]
    while len(blocks) > 1:
        blocks = [op(blocks[i], blocks[i + 1]) for i in range(0, len(blocks), 2)]
    return blocks[0]


def _rowmax(x):
    return jnp.max(_lane_fold(x, jnp.maximum), axis=1, keepdims=True)


def _rowsum(x):
    return jnp.sum(_lane_fold(x, jnp.add), axis=1, keepdims=True)


def _rope64(x, c, slo, shi):
    return x * c + pltpu.roll(x, 96, 1) * slo + pltpu.roll(x, 32, 1) * shi


def _rope128(x, c, s):
    return x * c + pltpu.roll(x, 64, 1) * s


def _norm_matmul_kernel(x_ref, g_ref, w_ref, o_ref, xn_ref):
    @pl.when(pl.program_id(1) == 0)
    def _():
        x = x_ref[...].astype(F32)
        ms = jnp.mean(x * x, axis=-1, keepdims=True)
        xn_ref[...] = (x * lax.rsqrt(ms + NORM_EPS) * g_ref[...]).astype(BF16)

    o_ref[...] = _dot(xn_ref[...], w_ref[...]).astype(o_ref.dtype)


def norm_matmul(x, g, w, *, k, xblk=0, tm=512, tn=None, out_dtype=F32):
    t = x.shape[0]
    n = w.shape[1]
    tn = n if tn is None else tn
    tm = min(tm, t)
    return pl.pallas_call(
        _norm_matmul_kernel,
        grid=(t // tm, n // tn),
        in_specs=[pl.BlockSpec((tm, k), lambda i, j: (i, xblk)),
                  pl.BlockSpec((1, k), lambda i, j: (0, 0)),
                  pl.BlockSpec((k, tn), lambda i, j: (0, j))],
        out_specs=pl.BlockSpec((tm, tn), lambda i, j: (i, j)),
        out_shape=jax.ShapeDtypeStruct((t, n), out_dtype),
        scratch_shapes=[pltpu.VMEM((tm, k), BF16)],
        compiler_params=_cparams("parallel", "arbitrary"),
        name="norm_matmul",
    )(x, g.reshape(1, k).astype(F32), w)


def _outproj_kernel(a1_ref, a2_ref, w1_ref, w2_ref, r_ref, o_ref):
    o_ref[...] = r_ref[...] + _dot(a1_ref[...], w1_ref[...]) + _dot(a2_ref[...], w2_ref[...])


def outproj_residual(a1, a2, w, res, *, tm=512, tn=2048):
    t, kh = a1.shape
    n = w.shape[1]
    tm = min(tm, t)
    return pl.pallas_call(
        _outproj_kernel,
        grid=(t // tm, n // tn),
        in_specs=[pl.BlockSpec((tm, kh), lambda i, j: (i, 0)),
                  pl.BlockSpec((tm, kh), lambda i, j: (i, 0)),
                  pl.BlockSpec((kh, tn), lambda i, j: (0, j)),
                  pl.BlockSpec((kh, tn), lambda i, j: (1, j)),
                  pl.BlockSpec((tm, tn), lambda i, j: (i, j))],
        out_specs=pl.BlockSpec((tm, tn), lambda i, j: (i, j)),
        out_shape=jax.ShapeDtypeStruct((t, n), F32),
        compiler_params=_cparams("parallel", "arbitrary"),
        name="outproj",
    )(a1, a2, w, w, res)


def _ple_kernel(h_ref, g_ref, wg_ref, p_ref, wp_ref, gout_ref, o_ref, *, norm_out):
    xn = _rms_normed(h_ref, g_ref).astype(BF16)
    gate = _sigmoid(_dot(xn, wg_ref[...]))
    emb = _dot(p_ref[...].astype(BF16), wp_ref[...])
    y = h_ref[...] + gate * emb
    o_ref[...] = _rms(y, gout_ref[...]) if norm_out else y


def ple_residual(h, g, wg, p, wp, g_out, *, norm_out, tm=512):
    t, d = h.shape
    pd = p.shape[1]
    tm = min(tm, t)
    return pl.pallas_call(
        functools.partial(_ple_kernel, norm_out=norm_out),
        grid=(t // tm,),
        in_specs=[pl.BlockSpec((tm, d), lambda i: (i, 0)),
                  pl.BlockSpec((1, d), lambda i: (0, 0)),
                  pl.BlockSpec((d, d), lambda i: (0, 0)),
                  pl.BlockSpec((tm, pd), lambda i: (i, 0)),
                  pl.BlockSpec((pd, d), lambda i: (0, 0)),
                  pl.BlockSpec((1, d), lambda i: (0, 0))],
        out_specs=pl.BlockSpec((tm, d), lambda i: (i, 0)),
        out_shape=jax.ShapeDtypeStruct((t, d), F32),
        compiler_params=_cparams("parallel"),
        name="ple",
    )(h, g.reshape(1, d).astype(F32), wg, p, wp, g_out.reshape(1, d).astype(F32))


def _rms(x, g):
    ms = jnp.mean(x * x, axis=-1, keepdims=True)
    return x * lax.rsqrt(ms + NORM_EPS) * g


def _rms_normed(h_ref, g_ref):
    return _rms(h_ref[...], g_ref[...])


def _route(xn, wr, br):
    n = xn.shape[0]
    x_hi, x_lo, _ = _split3(xn)
    w_hi, w_lo, _ = _split3(wr)
    logits = _dot(x_hi, w_hi) + (_dot(x_hi, w_lo) + _dot(x_lo, w_hi)) + br
    lane = lax.broadcasted_iota(I32, (n, LANE), 1)
    glog = jnp.where(lane < N_GROUPS, logits, NEG_INF)
    gmax = jnp.max(glog, axis=-1, keepdims=True)
    g_sel = jnp.min(jnp.where(glog == gmax, lane, LANE), axis=-1, keepdims=True)
    p_sel = 1.0 / jnp.sum(jnp.exp(glog - gmax), axis=-1, keepdims=True)
    eidx = lane - N_GROUPS
    in_grp = (eidx >= 0) & (eidx < N_EXPERTS) & ((eidx // EXPERTS_PER_GROUP) == g_sel)
    e1 = jnp.where(in_grp, logits, NEG_INF)
    v1 = jnp.max(e1, axis=-1, keepdims=True)
    i1 = jnp.min(jnp.where(e1 == v1, eidx, LANE), axis=-1, keepdims=True)
    e2 = jnp.where(eidx == i1, NEG_INF, e1)
    v2 = jnp.max(e2, axis=-1, keepdims=True)
    i2 = jnp.min(jnp.where(e2 == v2, eidx, LANE), axis=-1, keepdims=True)
    ex = jnp.exp(v2 - v1)
    return i1, i2, p_sel / (1.0 + ex), p_sel * ex / (1.0 + ex)


def _moe_kernel(h_ref, g_ref, wr_ref, br_ref, wgu_ref, wd_ref, o_ref, xn_ref, comb_ref, *, ke):
    e = pl.program_id(1)
    tm = h_ref.shape[0]

    @pl.when(e == 0)
    def _():
        xn = _rms_normed(h_ref, g_ref)
        xn_ref[...] = xn.astype(BF16)
        o_ref[...] = h_ref[...]
        i1, i2, w1, w2 = _route(xn, wr_ref[...], br_ref[...])
        lane = lax.broadcasted_iota(I32, (tm, LANE), 1)
        comb_ref[...] = jnp.where(lane == i1, w1, 0.0) + jnp.where(lane == i2, w2, 0.0)

    xn = xn_ref[...]
    lane = lax.broadcasted_iota(I32, (tm, LANE), 1)
    comb = comb_ref[...]
    hids = []
    for j in range(ke):
        gu = _dot(xn, wgu_ref[j])
        c = jnp.sum(jnp.where(lane == e * ke + j, comb, 0.0), axis=-1, keepdims=True)
        hid = _silu(gu[:, :EXPERT_FF]) * gu[:, EXPERT_FF:] * c
        hids.append(hid.astype(BF16))
    hid = jnp.concatenate(hids, axis=1)
    wd = wd_ref[...].reshape(ke * EXPERT_FF, wd_ref.shape[2])
    o_ref[...] += _dot(hid, wd)


def moe_residual(h, g, w_router, b_router, w_gu, w_down, *, tm=512, ke=4):
    t, d = h.shape
    tm = min(tm, t)
    return pl.pallas_call(
        functools.partial(_moe_kernel, ke=ke),
        grid=(t // tm, N_EXPERTS // ke),
        in_specs=[pl.BlockSpec((tm, d), lambda i, e: (i, 0)),
                  pl.BlockSpec((1, d), lambda i, e: (0, 0)),
                  pl.BlockSpec((d, LANE), lambda i, e: (0, 0)),
                  pl.BlockSpec((1, LANE), lambda i, e: (0, 0)),
                  pl.BlockSpec((ke, d, 2 * EXPERT_FF), lambda i, e: (e, 0, 0)),
                  pl.BlockSpec((ke, EXPERT_FF, d), lambda i, e: (e, 0, 0))],
        out_specs=pl.BlockSpec((tm, d), lambda i, e: (i, 0)),
        out_shape=jax.ShapeDtypeStruct((t, d), F32),
        scratch_shapes=[pltpu.VMEM((tm, d), BF16), pltpu.VMEM((tm, LANE), F32)],
        compiler_params=_cparams("parallel", "arbitrary"),
        name="moe",
    )(h, g.reshape(1, d).astype(F32), w_router, b_router, w_gu, w_down)


MOE_SUB = 256
MOE_CAP = 48
MOE_SLOTS = N_EXPERTS * MOE_CAP


def _moe_gather_kernel(h_ref, g_ref, wr_ref, br_ref, xc_ref, meta_ref, flag_ref):
    n = h_ref.shape[0]
    xn = _rms_normed(h_ref, g_ref)
    i1, i2, w1, w2 = _route(xn, wr_ref[...], br_ref[...])
    lane = lax.broadcasted_iota(I32, (n, LANE), 1)
    onehot = jnp.where((lane == i1) | (lane == i2), 1.0, 0.0)
    earlier = (lax.broadcasted_iota(I32, (n, n), 1) < lax.broadcasted_iota(I32, (n, n), 0)).astype(BF16)
    rank = _dot(earlier, onehot.astype(BF16))
    r1 = jnp.sum(jnp.where(lane == i1, rank, 0.0), axis=-1, keepdims=True)
    r2 = jnp.sum(jnp.where(lane == i2, rank, 0.0), axis=-1, keepdims=True)
    fits = (r1 < MOE_CAP) & (r2 < MOE_CAP)
    flag_ref[...] = jnp.broadcast_to(jnp.max(jnp.where(fits, 0.0, 1.0), axis=0, keepdims=True), flag_ref.shape)
    pos1 = jnp.where(r1 < MOE_CAP, i1.astype(F32) * MOE_CAP + r1, -1.0)
    pos2 = jnp.where(r2 < MOE_CAP, i2.astype(F32) * MOE_CAP + r2, -1.0)
    meta = (jnp.where(lane == 0, pos1, 0.0) + jnp.where(lane == 1, pos2, 0.0)
            + jnp.where(lane == 2, w1, 0.0) + jnp.where(lane == 3, w2, 0.0))
    meta_ref[...] = meta
    meta_t = meta.T
    slot = lax.broadcasted_iota(I32, (MOE_SLOTS, n), 0).astype(F32)
    pick = jnp.where((slot == meta_t[0:1, :]) | (slot == meta_t[1:2, :]), 1.0, 0.0).astype(BF16)
    xc = _dot(pick, xn.astype(BF16)).astype(BF16)
    xc_ref[...] = xc.reshape(xc_ref.shape)


def _moe_ffn_kernel(x_ref, wg_ref, wu_ref, wd_ref, y_ref):
    x = x_ref[...]
    hid = _silu(_dot(x, wg_ref[...].astype(BF16))) * _dot(x, wu_ref[...].astype(BF16))
    y_ref[...] = _dot(hid.astype(BF16), wd_ref[...].astype(BF16)).astype(y_ref.dtype)


def _moe_scatter_kernel(y_ref, meta_ref, h_ref, o_ref):
    n = h_ref.shape[0]
    meta = meta_ref[...]
    slot = lax.broadcasted_iota(I32, (n, MOE_SLOTS), 1).astype(F32)
    y = y_ref[...].reshape(MOE_SLOTS, y_ref.shape[-1])
    pick = (jnp.where(slot == meta[:, 0:1], meta[:, 2:3], 0.0)
            + jnp.where(slot == meta[:, 1:2], meta[:, 3:4], 0.0)).astype(BF16)
    o_ref[...] = h_ref[...] + _dot(pick, y)


def moe_routed(h, g, w_router, b_router, w_gate, w_up, w_down, layer, *, tm=1024):
    t, d = h.shape
    nsub = t // MOE_SUB
    rows = nsub * MOE_CAP
    tm = min(tm, rows)
    xc, meta, flag = pl.pallas_call(
        _moe_gather_kernel,
        grid=(nsub,),
        in_specs=[pl.BlockSpec((MOE_SUB, d), lambda i: (i, 0)),
                  pl.BlockSpec((1, d), lambda i: (0, 0)),
                  pl.BlockSpec((d, LANE), lambda i: (0, 0)),
                  pl.BlockSpec((1, LANE), lambda i: (0, 0))],
        out_specs=[pl.BlockSpec((N_EXPERTS, None, MOE_CAP, d), lambda i: (0, i, 0, 0)),
                   pl.BlockSpec((MOE_SUB, LANE), lambda i: (i, 0)),
                   pl.BlockSpec((None, 8, LANE), lambda i: (i, 0, 0))],
        out_shape=[jax.ShapeDtypeStruct((N_EXPERTS, nsub, MOE_CAP, d), BF16),
                   jax.ShapeDtypeStruct((t, LANE), F32),
                   jax.ShapeDtypeStruct((nsub, 8, LANE), F32)],
        compiler_params=_cparams("parallel"),
        name="moe_gather",
    )(h, g.reshape(1, d).astype(F32), w_router, b_router)
    y = pl.pallas_call(
        _moe_ffn_kernel,
        grid=(N_EXPERTS, rows // tm),
        in_specs=[pl.BlockSpec((None, tm, d), lambda e, i: (e, i, 0)),
                  pl.BlockSpec((None, None, d, EXPERT_FF), lambda e, i: (layer, e, 0, 0)),
                  pl.BlockSpec((None, None, d, EXPERT_FF), lambda e, i: (layer, e, 0, 0)),
                  pl.BlockSpec((None, None, EXPERT_FF, d), lambda e, i: (layer, e, 0, 0))],
        out_specs=pl.BlockSpec((None, tm, d), lambda e, i: (e, i, 0)),
        out_shape=jax.ShapeDtypeStruct((N_EXPERTS, rows, d), BF16),
        compiler_params=_cparams("parallel", "arbitrary"),
        name="moe_ffn",
    )(xc.reshape(N_EXPERTS, rows, d), w_gate, w_up, w_down)
    out = pl.pallas_call(
        _moe_scatter_kernel,
        grid=(nsub,),
        in_specs=[pl.BlockSpec((N_EXPERTS, None, MOE_CAP, d), lambda i: (0, i, 0, 0)),
                  pl.BlockSpec((MOE_SUB, LANE), lambda i: (i, 0)),
                  pl.BlockSpec((MOE_SUB, d), lambda i: (i, 0))],
        out_specs=pl.BlockSpec((MOE_SUB, d), lambda i: (i, 0)),
        out_shape=jax.ShapeDtypeStruct((t, d), F32),
        compiler_params=_cparams("parallel"),
        name="moe_scatter",
    )(y.reshape(N_EXPERTS, nsub, MOE_CAP, d), meta, h)
    return out, jnp.max(flag)


def moe_layer(h, g, w_router, b_router, w_gate, w_up, w_down, layer):
    out, overflow = moe_routed(h, g, w_router, b_router, w_gate, w_up, w_down, layer)

    def dense():
        w_gu = jnp.concatenate([w_gate[layer], w_up[layer]], axis=-1).astype(BF16)
        return moe_residual(h, g, w_router, b_router, w_gu, w_down[layer].astype(BF16))

    return lax.cond(overflow > 0.0, dense, lambda: out)


def _dsa_prep_kernel(x_ref, ca_ref, sa_ref, ci_ref, slo_ref, shi_ref,
                     q_ref, k_ref, v_ref, iq_ref, ik_ref, iw_ref):
    ca, sa = ca_ref[...], sa_ref[...]
    ci, slo, shi = ci_ref[...], slo_ref[...], shi_ref[...]
    scale = A_HEAD_DIM ** -0.5 * math.log2(math.e)
    for h in range(A_HEADS):
        xh = x_ref[:, h * LANE:(h + 1) * LANE]
        q_ref[:, h * LANE:(h + 1) * LANE] = (_rope128(xh, ca, sa) * scale).astype(BF16)
    off = A_HEADS * A_HEAD_DIM
    for h in range(A_KV_HEADS):
        xh = x_ref[:, off + h * LANE:off + (h + 1) * LANE]
        k_ref[:, h * LANE:(h + 1) * LANE] = _rope128(xh, ca, sa).astype(BF16)
    off += A_KV_HEADS * A_HEAD_DIM
    v_ref[...] = x_ref[:, off:off + A_KV_HEADS * A_HEAD_DIM].astype(BF16)
    off += A_KV_HEADS * A_HEAD_DIM
    for h in range(IDX_HEADS * IDX_DIM // LANE):
        xh = x_ref[:, off + h * LANE:off + (h + 1) * LANE]
        iq_ref[:, h * LANE:(h + 1) * LANE] = _rope64(xh, ci, slo, shi).astype(BF16)
    off += IDX_HEADS * IDX_DIM
    misc = x_ref[:, off:off + LANE]
    ik_ref[...] = _rope64(misc, ci, slo, shi)[:, :IDX_DIM].astype(BF16)
    iw_ref[...] = misc[:, IDX_DIM:IDX_DIM + IDX_HEADS] * IDX_SCALE


def dsa_prep(proj, tabs, *, tm=512):
    t, n = proj.shape
    tm = min(tm, t)
    row = lambda w: pl.BlockSpec((tm, w), lambda i: (i, 0))
    return pl.pallas_call(
        _dsa_prep_kernel,
        grid=(t // tm,),
        in_specs=[row(n)] + [row(LANE)] * 5,
        out_specs=[row(1024), row(256), row(256), row(512), row(IDX_DIM), row(IDX_HEADS)],
        out_shape=[jax.ShapeDtypeStruct((t, 1024), BF16), jax.ShapeDtypeStruct((t, 256), BF16),
                   jax.ShapeDtypeStruct((t, 256), BF16), jax.ShapeDtypeStruct((t, 512), BF16),
                   jax.ShapeDtypeStruct((t, IDX_DIM), BF16), jax.ShapeDtypeStruct((t, IDX_HEADS), F32)],
        compiler_params=_cparams("parallel"),
        name="dsa_prep",
    )(proj, *tabs)


def _dsa_kernel(q_ref, k_ref, v_ref, iq_ref, ik_ref, iw_ref, o_ref,
                key_ref, m_ref, l_ref, acc_ref, *, tk, topk):
    qi = pl.program_id(1)
    nq = q_ref.shape[0]
    rep = A_HEADS // A_KV_HEADS
    n_tiles = ((qi + 1) * nq + tk - 1) // tk
    row = lax.broadcasted_iota(I32, (nq, tk), 0)
    col = lax.broadcasted_iota(I32, (nq, tk), 1)
    q_chunk = (qi * nq + row) // CHUNK

    def admissible(off):
        return ((off + col) // CHUNK) <= q_chunk

    iq = iq_ref[...]
    iw = iw_ref[...]

    def score_body(t, carry):
        off = pl.multiple_of(t * tk, tk)
        ikt = ik_ref[pl.ds(off, tk), :]
        s = jnp.zeros((nq, tk), F32)
        for h in range(IDX_HEADS):
            rel = _dot_nt(iq[:, h * IDX_DIM:(h + 1) * IDX_DIM], ikt)
            s = s + iw[:, h:h + 1] * jnp.maximum(rel, 0.0)
        key_ref[:, pl.ds(off, tk)] = jnp.where(admissible(off), s, NEG_INF)
        return carry

    lax.fori_loop(0, n_tiles, score_body, 0)

    def scan(fn, init):
        def body(t, acc):
            off = pl.multiple_of(t * tk, tk)
            blk = key_ref[:, pl.ds(off, tk)]
            for c in range(tk // LANE):
                acc = fn(blk[:, c * LANE:(c + 1) * LANE], acc)
            return acc
        return lax.fori_loop(0, n_tiles, body, init)

    def rowsum(x):
        return jnp.sum(x, axis=1, keepdims=True)

    def count_ge(v):
        return rowsum(scan(lambda blk, acc: acc + jnp.where(blk >= v, 1.0, 0.0), jnp.zeros((nq, LANE), F32)))

    big = jnp.full((nq, LANE), -NEG_INF, F32)
    zeros = jnp.zeros((nq, LANE), F32)
    lo_p, hi_p, n_p = scan(
        lambda blk, acc: (jnp.minimum(acc[0], jnp.where(blk > NEG_INF, blk, big)), jnp.maximum(acc[1], blk),
                          acc[2] + jnp.where(blk > NEG_INF, 1.0, 0.0)), (big, -big, zeros))
    enough = rowsum(n_p) >= topk
    lo0 = jnp.where(enough, jnp.min(lo_p, axis=1, keepdims=True), NEG_INF)
    hi0 = jnp.where(enough, jnp.max(hi_p, axis=1, keepdims=True), NEG_INF)

    def bisect_body(i, bracket):
        lo, hi = bracket
        mid = lo + 0.5 * (hi - lo)
        ge = count_ge(mid) >= topk
        return jnp.where(ge, mid, lo), jnp.where(ge, hi, mid)

    _, hi = lax.fori_loop(0, BISECT_STEPS, bisect_body, (lo0, hi0))

    def walk_cond(state):
        return state[1] > 0

    def walk_body(state):
        v, _ = state
        zero = jnp.zeros((nq, LANE), F32)
        cnt_p, nxt_p = scan(lambda blk, acc: (acc[0] + jnp.where(blk >= v, 1.0, 0.0),
                                              jnp.maximum(acc[1], jnp.where(blk < v, blk, -big))),
                            (zero, -big))
        found = rowsum(cnt_p) >= topk
        v = jnp.where(found, v, jnp.max(nxt_p, axis=1, keepdims=True))
        return v, jnp.sum(jnp.where(found, 0, 1))

    hi_val = jnp.max(scan(lambda blk, acc: jnp.maximum(acc, jnp.where(blk <= hi, blk, -big)), -big),
                     axis=1, keepdims=True)
    thr, _ = lax.while_loop(walk_cond, walk_body, (hi_val, jnp.int32(1)))
    n_gt = rowsum(scan(lambda blk, acc: acc + jnp.where(blk > thr, 1.0, 0.0), jnp.zeros((nq, LANE), F32)))
    need = topk - n_gt

    m_ref[...] = jnp.full(m_ref.shape, NEG_INF, F32)
    l_ref[...] = jnp.zeros(l_ref.shape, F32)
    acc_ref[...] = jnp.zeros(acc_ref.shape, F32)
    upper = (lax.broadcasted_iota(I32, (LANE, LANE), 0)
             < lax.broadcasted_iota(I32, (LANE, LANE), 1)).astype(BF16)
    qg = [jnp.concatenate([q_ref[:, (g * rep + r) * LANE:(g * rep + r + 1) * LANE]
                           for r in range(rep)], axis=0) for g in range(A_KV_HEADS)]

    def attn_body(t, run):
        off = pl.multiple_of(t * tk, tk)
        keyt = key_ref[:, pl.ds(off, tk)]
        adm = admissible(off)
        sels = []
        for c in range(tk // LANE):
            kc = keyt[:, c * LANE:(c + 1) * LANE]
            eqf = jnp.where(kc == thr, 1.0, 0.0)
            before = _dot(eqf.astype(BF16), upper) + run
            take = jnp.where(kc > thr, 1.0, jnp.where(before < need, eqf, 0.0))
            sels.append(take)
            run = run + jnp.sum(eqf, axis=1, keepdims=True)
        sel = jnp.where(adm, jnp.concatenate(sels, axis=1), 0.0) > 0.5
        sel_r = jnp.concatenate([sel] * rep, axis=0)
        groups = range(A_KV_HEADS)
        ss = [jnp.where(sel_r, _dot_nt(qg[g], k_ref[pl.ds(off, tk), g * LANE:(g + 1) * LANE]), NEG_INF)
              for g in groups]
        m_old = [m_ref[g] for g in groups]
        m_new = [jnp.maximum(m_old[g], _rowmax(ss[g])) for g in groups]
        ps = [jnp.exp2(ss[g] - m_new[g]) for g in groups]
        pv = [_dot(ps[g].astype(BF16), v_ref[pl.ds(off, tk), g * LANE:(g + 1) * LANE]) for g in groups]
        for g in groups:
            alpha = jnp.exp2(m_old[g] - m_new[g])
            l_ref[g] = alpha * l_ref[g] + _rowsum(ps[g])
            acc_ref[g] = alpha * acc_ref[g] + pv[g]
            m_ref[g] = m_new[g]
        return run

    lax.fori_loop(0, n_tiles, attn_body, jnp.zeros((nq, 1), F32))
    for g in range(A_KV_HEADS):
        out = acc_ref[g] / l_ref[g]
        for r in range(rep):
            h = g * rep + r
            o_ref[:, h * LANE:(h + 1) * LANE] = out[r * nq:(r + 1) * nq].astype(o_ref.dtype)


def dsa_attention(q, k, v, iq, ik, iw, *, nq=128, tk=1024):
    b, s, _ = q.shape
    tk = min(tk, s)
    topk = min(TOPK_MAX, s // 4)
    rep = A_HEADS // A_KV_HEADS
    qspec = lambda w: pl.BlockSpec((None, nq, w), lambda bi, qi: (bi, qi, 0))
    kspec = lambda w: pl.BlockSpec((None, s, w), lambda bi, qi: (bi, 0, 0))
    return pl.pallas_call(
        functools.partial(_dsa_kernel, tk=tk, topk=topk),
        grid=(b, s // nq),
        in_specs=[qspec(1024), kspec(256), kspec(256), qspec(512), kspec(IDX_DIM), qspec(IDX_HEADS)],
        out_specs=qspec(1024),
        out_shape=jax.ShapeDtypeStruct((b, s, 1024), BF16),
        scratch_shapes=[pltpu.VMEM((nq, s), F32),
                        pltpu.VMEM((A_KV_HEADS, rep * nq, 1), F32),
                        pltpu.VMEM((A_KV_HEADS, rep * nq, 1), F32),
                        pltpu.VMEM((A_KV_HEADS, rep * nq, LANE), F32)],
        compiler_params=_cparams("parallel", "arbitrary"),
        name="dsa_attention",
    )(q, k, v, iq, ik, iw)


def _causal_conv(x, tail_ref, xp_ref, w):
    r = x.shape[0]
    width = w.shape[0]
    xp_ref[0:8, :] = tail_ref[...]
    xp_ref[8:8 + r, :] = x
    tail_ref[...] = x[r - 8:r, :]
    acc = x * w[width - 1:width, :]
    for j in range(width - 1):
        acc = acc + xp_ref[pl.ds(8 - (width - 1) + j, r), :] * w[j:j + 1, :]
    return acc


def _unit_lower_inverse(mats):
    n = mats[0].shape[0]
    eye = (lax.broadcasted_iota(I32, (n, n), 0) == lax.broadcasted_iota(I32, (n, n), 1)).astype(F32)
    ps = [-a for a in mats]
    ts = [eye + p for p in ps]
    for _ in range(int(math.ceil(math.log2(n))) - 1):
        ps = [_bdot(p, p) for p in ps]
        ts = [t + _bdot(t, p) for t, p in zip(ts, ps)]
    return ts


def _gdn_kernel(qkv_ref, z_ref, misc_ref, convw_ref, alog_ref, dtb_ref, nw_ref, o_ref,
                tail_ref, xp_ref, q_s, k_s, v_s, beta_s, g_s, state_ref):
    rows = qkv_ref.shape[0]
    hd = B_HEAD_DIM
    nh = B_HEADS

    @pl.when(pl.program_id(1) == 0)
    def _():
        tail_ref[...] = jnp.zeros(tail_ref.shape, F32)
        state_ref[...] = jnp.zeros(state_ref.shape, F32)

    y = _silu(_causal_conv(qkv_ref[...], tail_ref, xp_ref, convw_ref[...]))
    for h in range(nh):
        qh = y[:, h * hd:(h + 1) * hd]
        kh = y[:, (nh + h) * hd:(nh + h + 1) * hd]
        q_s[:, h * hd:(h + 1) * hd] = qh * lax.rsqrt(jnp.sum(qh * qh, -1, keepdims=True) + 1e-6) * hd ** -0.5
        k_s[:, h * hd:(h + 1) * hd] = kh * lax.rsqrt(jnp.sum(kh * kh, -1, keepdims=True) + 1e-6)
    v_s[...] = y[:, 2 * nh * hd:]
    misc = misc_ref[...]
    beta_s[...] = _sigmoid(misc)
    g_s[...] = -jnp.exp(alog_ref[...]) * _softplus(misc + dtb_ref[...])

    ri = lax.broadcasted_iota(I32, (CHUNK, CHUNK), 0)
    ci = lax.broadcasted_iota(I32, (CHUNK, CHUNK), 1)
    incl = ri >= ci
    strict = ri > ci
    lower01 = incl.astype(BF16)
    nw = nw_ref[...]

    def chunk_body(c, carry):
        r0 = pl.multiple_of(c * CHUNK, CHUNK)
        rs = pl.ds(r0, CHUNK)
        gc = _dot01_left(lower01, g_s[rs, :])
        gct = gc.T
        beta = beta_s[rs, :]
        eg = jnp.exp(gc)
        g_last = gc[CHUNK - 1:CHUNK, :]
        e_last = jnp.exp(g_last)
        e_rem = jnp.exp(g_last - gc)
        heads = range(nh)
        hs = [slice(h * hd, (h + 1) * hd) for h in heads]
        col = lambda x, h: x[:, nh + h:nh + h + 1]
        qh = [q_s[rs, hs[h]] for h in heads]
        kh = [k_s[rs, hs[h]] for h in heads]
        bcol = [beta[:, h:h + 1] for h in heads]
        kb = [kh[h] * bcol[h] for h in heads]
        kk = [_bdot_nt(kb[h], kh[h]) for h in heads]
        qk = [_bdot_nt(qh[h], kh[h]) for h in heads]
        st = [state_ref[h] for h in heads]
        q_st = [_bdot(qh[h] * col(eg, h), st[h]) for h in heads]
        decay = []
        for h in heads:
            diff = col(gc, h) - gct[nh + h:nh + h + 1, :]
            decay.append(jnp.where(incl, jnp.exp(jnp.where(incl, diff, 0.0)), 0.0))
        t_mat = _unit_lower_inverse([jnp.where(strict, kk[h] * decay[h], 0.0) for h in heads])
        uw = [_bdot(t_mat[h], jnp.concatenate([v_s[rs, hs[h]] * bcol[h], kb[h] * col(eg, h)], axis=1))
              for h in heads]
        w_st = [_bdot(uw[h][:, hd:], st[h]) for h in heads]
        v_new = [uw[h][:, :hd] - w_st[h] for h in heads]
        o = [q_st[h] + _bdot(qk[h] * decay[h], v_new[h]) for h in heads]
        upd = [_bdot((kh[h] * col(e_rem, h)).T, v_new[h]) for h in heads]
        for h in heads:
            state_ref[h] = st[h] * col(e_last, h) + upd[h]
            on = o[h] * lax.rsqrt(jnp.mean(o[h] * o[h], -1, keepdims=True) + NORM_EPS) * nw
            o_ref[rs, hs[h]] = (on * _silu(z_ref[rs, hs[h]])).astype(o_ref.dtype)
        return carry

    lax.fori_loop(0, rows // CHUNK, chunk_body, 0)


def gdn_mixer(proj, conv_w, alog_row, dtb_row, norm_w, *, batch, rows=256):
    t = proj.shape[0]
    s = t // batch
    rows = min(rows, s)
    nr = s // rows
    nqkv = 3 * B_HEADS * B_HEAD_DIM
    nz = B_HEADS * B_HEAD_DIM
    const = lambda shape: pl.BlockSpec(shape, lambda b, r: (0, 0))
    return pl.pallas_call(
        _gdn_kernel,
        grid=(batch, nr),
        in_specs=[pl.BlockSpec((rows, nqkv), lambda b, r: (b * nr + r, 0)),
                  pl.BlockSpec((rows, nz), lambda b, r: (b * nr + r, nqkv // nz)),
                  pl.BlockSpec((rows, LANE), lambda b, r: (b * nr + r, (nqkv + nz) // LANE)),
                  const((CONV_WIDTH, nqkv)), const((1, LANE)), const((1, LANE)), const((1, LANE))],
        out_specs=pl.BlockSpec((rows, nz), lambda b, r: (b * nr + r, 0)),
        out_shape=jax.ShapeDtypeStruct((t, nz), BF16),
        scratch_shapes=[pltpu.VMEM((8, nqkv), F32), pltpu.VMEM((rows + 8, nqkv), F32),
                        pltpu.VMEM((rows, nz), F32), pltpu.VMEM((rows, nz), F32), pltpu.VMEM((rows, nz), F32),
                        pltpu.VMEM((rows, LANE), F32), pltpu.VMEM((rows, LANE), F32),
                        pltpu.VMEM((B_HEADS, B_HEAD_DIM, B_HEAD_DIM), F32)],
        compiler_params=_cparams("parallel", "arbitrary"),
        name="gdn",
    )(proj, proj, proj, conv_w, alog_row, dtb_row, norm_w)


def _mla_prep_kernel(q_ref, kr_ref, ci_ref, slo_ref, shi_ref, qo_ref, kro_ref):
    ci, slo, shi = ci_ref[...], slo_ref[...], shi_ref[...]
    scale = (C_NOPE + C_ROPE) ** -0.5 * math.log2(math.e)
    for h in range(C_HEADS):
        base = h * 2 * LANE
        qo_ref[:, base:base + LANE] = (q_ref[:, base:base + LANE] * scale).astype(BF16)
        qo_ref[:, base + LANE:base + 2 * LANE] = (
            _rope64(q_ref[:, base + LANE:base + 2 * LANE], ci, slo, shi) * scale).astype(BF16)
    kro_ref[...] = _rope64(kr_ref[...], ci, slo, shi).astype(BF16)


def mla_prep(q_raw, proj_c, tabs, *, tm=512):
    t, n = q_raw.shape
    tm = min(tm, t)
    row = lambda w: pl.BlockSpec((tm, w), lambda i: (i, 0))
    return pl.pallas_call(
        _mla_prep_kernel,
        grid=(t // tm,),
        in_specs=[row(n), pl.BlockSpec((tm, LANE), lambda i: (i, (C_Q_RANK + C_KV_RANK) // LANE)),
                  row(LANE), row(LANE), row(LANE)],
        out_specs=[row(n), row(LANE)],
        out_shape=[jax.ShapeDtypeStruct((t, n), BF16), jax.ShapeDtypeStruct((t, LANE), BF16)],
        compiler_params=_cparams("parallel"),
        name="mla_prep",
    )(q_raw, proj_c, *tabs)


def _mla_kernel(q_ref, kv_ref, kr_ref, o_ref, m_ref, l_ref, acc_ref, *, hb):
    qi = pl.program_id(2)
    tq = q_ref.shape[0]
    heads = range(hb)
    row = lax.broadcasted_iota(I32, (tq, tq), 0)
    col = lax.broadcasted_iota(I32, (tq, tq), 1)
    diag_mask = (col // CHUNK) <= (row // CHUNK)
    m_ref[...] = jnp.full(m_ref.shape, NEG_INF, F32)
    l_ref[...] = jnp.zeros(l_ref.shape, F32)
    acc_ref[...] = jnp.zeros(acc_ref.shape, F32)
    qs = [q_ref[:, h * 2 * LANE:(h + 1) * 2 * LANE] for h in heads]

    def step(off, on_diagonal):
        rows = pl.ds(off, tq)
        kr = kr_ref[rows, :]
        def scores(h):
            s = _dot_nt(qs[h], jnp.concatenate([kv_ref[rows, h * 2 * LANE:h * 2 * LANE + LANE], kr], axis=1))
            return jnp.where(diag_mask, s, NEG_INF) if on_diagonal else s

        def update(h, s):
            m_old = m_ref[h]
            m_new = jnp.maximum(m_old, _rowmax(s))
            p = jnp.exp2(s - m_new)
            pv = _dot(p.astype(BF16), kv_ref[rows, h * 2 * LANE + LANE:(h + 1) * 2 * LANE])
            alpha = jnp.exp2(m_old - m_new)
            l_ref[h] = alpha * l_ref[h] + _rowsum(p)
            acc_ref[h] = alpha * acc_ref[h] + pv
            m_ref[h] = m_new

        s_next = scores(0)
        for h in heads:
            s_cur = s_next
            if h + 1 < hb:
                s_next = scores(h + 1)
            update(h, s_cur)

    def body(t, carry):
        step(pl.multiple_of(t * tq, tq), False)
        return carry

    lax.fori_loop(0, qi, body, 0)
    step(pl.multiple_of(qi * tq, tq), True)
    for h in heads:
        o_ref[:, h * LANE:(h + 1) * LANE] = (acc_ref[h] / l_ref[h]).astype(o_ref.dtype)


def mla_attention(q, kv, kr, *, tq=1024, hb=2):
    b, s, _ = q.shape
    tq = min(tq, s)
    return pl.pallas_call(
        functools.partial(_mla_kernel, hb=hb),
        grid=(b, C_HEADS // hb, s // tq),
        in_specs=[pl.BlockSpec((None, tq, hb * 2 * LANE), lambda bi, h, qi: (bi, qi, h)),
                  pl.BlockSpec((None, s, hb * 2 * LANE), lambda bi, h, qi: (bi, 0, h)),
                  pl.BlockSpec((None, s, LANE), lambda bi, h, qi: (bi, 0, 0))],
        out_specs=pl.BlockSpec((None, tq, hb * LANE), lambda bi, h, qi: (bi, qi, h)),
        out_shape=jax.ShapeDtypeStruct((b, s, C_HEADS * C_V), BF16),
        scratch_shapes=[pltpu.VMEM((hb, tq, 1), F32), pltpu.VMEM((hb, tq, 1), F32),
                        pltpu.VMEM((hb, tq, LANE), F32)],
        compiler_params=_cparams("parallel", "parallel", "arbitrary"),
        name="mla_attention",
    )(q, kv, kr)


def _ssd_kernel(zx_ref, misc_ref, convw_ref, convb_ref, dtb_ref, a_ref, aexp_ref, dskip_ref,
                nw_ref, expand_ref, o_ref,
                tail_ref, xp_ref, xs_s, xdt_s, bm_s, cm_s, la_s, laexp_s, y_s, state_ref):
    rows = zx_ref.shape[0]
    pd = D_HEAD_DIM
    gw = D_INNER // D_GROUPS
    pairs_per_group = gw // LANE

    @pl.when(pl.program_id(1) == 0)
    def _():
        tail_ref[...] = jnp.zeros(tail_ref.shape, F32)
        state_ref[...] = jnp.zeros(state_ref.shape, F32)

    y = _silu(_causal_conv(zx_ref[:, D_INNER:], tail_ref, xp_ref, convw_ref[...]) + convb_ref[...])
    xs = y[:, :D_INNER]
    xs_s[...] = xs
    bm_s[...] = y[:, D_INNER:D_INNER + D_GROUPS * D_STATE]
    cm_s[...] = y[:, D_INNER + D_GROUPS * D_STATE:]
    dt = _softplus(misc_ref[...] + dtb_ref[...])
    dt_exp = _dot01_right(dt, expand_ref[...])
    xdt_s[...] = xs * dt_exp
    la_s[...] = dt * a_ref[...]
    laexp_s[...] = dt_exp * aexp_ref[...]

    ri = lax.broadcasted_iota(I32, (CHUNK, CHUNK), 0)
    ci = lax.broadcasted_iota(I32, (CHUNK, CHUNK), 1)
    incl = ri >= ci
    lower01 = incl.astype(BF16)
    lane_lo = lax.broadcasted_iota(I32, (CHUNK, LANE), 1) < pd

    def chunk_body(c, carry):
        r0 = pl.multiple_of(c * CHUNK, CHUNK)
        rs = pl.ds(r0, CHUNK)
        cum = _dot01_left(lower01, la_s[rs, :])
        cumt = cum.T
        cum_e = _dot01_left(lower01, laexp_s[rs, :])
        ecum = jnp.exp(cum_e)
        last = cum_e[CHUNK - 1:CHUNK, :]
        dec_st = jnp.exp(last - cum_e)
        cdec = jnp.exp(last)
        for g in range(D_GROUPS):
            bm = bm_s[rs, g * D_STATE:(g + 1) * D_STATE]
            cm = cm_s[rs, g * D_STATE:(g + 1) * D_STATE]
            cb = _bdot_nt(cm, bm)
            bmt = bm.T.astype(BF16)
            gs = slice(g * gw, (g + 1) * gw)
            y_off = _bdot(cm, state_ref[:, gs]) * ecum[:, gs]
            for pp in range(pairs_per_group):
                pidx = g * pairs_per_group + pp
                ps = slice(pidx * LANE, (pidx + 1) * LANE)
                xp = xdt_s[rs, ps]
                ys = []
                for hh in range(2):
                    h = 2 * pidx + hh
                    diff = cum[:, h:h + 1] - cumt[h:h + 1, :]
                    lmat = jnp.where(incl, jnp.exp(jnp.where(incl, diff, 0.0)), 0.0)
                    ys.append(_bdot(cb * lmat, xp))
                y_diag = jnp.where(lane_lo, ys[0], ys[1])
                y_s[rs, ps] = (y_diag + y_off[:, pp * LANE:(pp + 1) * LANE]
                               + dskip_ref[:, ps] * xs_s[rs, ps])
                s_new = _dot(bmt, (xp * dec_st[:, ps]).astype(BF16))
                state_ref[:, ps] = state_ref[:, ps] * cdec[:, ps] + s_new
        return carry

    lax.fori_loop(0, rows // CHUNK, chunk_body, 0)
    yv = y_s[...] * _silu(zx_ref[:, :D_INNER])
    for g in range(D_GROUPS):
        gs = slice(g * gw, (g + 1) * gw)
        yg = yv[:, gs]
        ms = jnp.mean(yg * yg, axis=-1, keepdims=True)
        o_ref[:, gs] = (yg * lax.rsqrt(ms + NORM_EPS) * nw_ref[:, gs]).astype(o_ref.dtype)


def ssd_mixer(proj, conv_w, conv_b, dtb_row, a_row, a_exp, dskip_exp, norm_w, expand, *, batch, rows=256):
    t = proj.shape[0]
    s = t // batch
    rows = min(rows, s)
    nr = s // rows
    nx = D_INNER + 2 * D_GROUPS * D_STATE
    const = lambda shape: pl.BlockSpec(shape, lambda b, r: (0, 0))
    return pl.pallas_call(
        _ssd_kernel,
        grid=(batch, nr),
        in_specs=[pl.BlockSpec((rows, D_INNER + nx), lambda b, r: (b * nr + r, 0)),
                  pl.BlockSpec((rows, LANE), lambda b, r: (b * nr + r, (D_INNER + nx) // LANE)),
                  const((CONV_WIDTH, nx)), const((1, nx)), const((1, LANE)), const((1, LANE)),
                  const((1, D_INNER)), const((1, D_INNER)), const((1, D_INNER)), const((LANE, D_INNER))],
        out_specs=pl.BlockSpec((rows, D_INNER), lambda b, r: (b * nr + r, 0)),
        out_shape=jax.ShapeDtypeStruct((t, D_INNER), BF16),
        scratch_shapes=[pltpu.VMEM((8, nx), F32), pltpu.VMEM((rows + 8, nx), F32),
                        pltpu.VMEM((rows, D_INNER), F32), pltpu.VMEM((rows, D_INNER), F32),
                        pltpu.VMEM((rows, D_GROUPS * D_STATE), F32), pltpu.VMEM((rows, D_GROUPS * D_STATE), F32),
                        pltpu.VMEM((rows, LANE), F32), pltpu.VMEM((rows, D_INNER), F32),
                        pltpu.VMEM((rows, D_INNER), F32),
                        pltpu.VMEM((D_STATE, D_INNER), F32)],
        compiler_params=_cparams("parallel", "arbitrary"),
        name="ssd",
    )(proj, proj, conv_w, conv_b, dtb_row, a_row, a_exp, dskip_exp, norm_w, expand)


def _rope_tables(positions):
    pos = positions.reshape(-1).astype(F32)[:, None]

    def cs(dim):
        inv = jnp.power(ROPE_THETA, -jnp.arange(0, dim, 2, dtype=F32) / dim)
        ang = pos * inv
        return jnp.cos(ang), jnp.sin(ang)

    c, s = cs(A_HEAD_DIM)
    tab_a = (jnp.concatenate([c, c], -1), jnp.concatenate([-s, s], -1))
    c, s = cs(IDX_DIM)
    z = jnp.zeros_like(s)
    tab_i = (jnp.concatenate([c, c, c, c], -1), jnp.concatenate([-s, z, -s, z], -1),
             jnp.concatenate([z, s, z, s], -1))
    return tab_a, tab_i


def _pad_cols(w, n):
    return jnp.pad(w, ((0, 0), (0, n - w.shape[1])))


def _lane_row(v, start):
    return jnp.zeros((1, LANE), F32).at[0, start:start + v.shape[0]].set(v.astype(F32))


def _moe_weights(i, w_group, b_group, w_expert, b_expert, w_gate, w_up, w_down):
    w_router = _pad_cols(jnp.concatenate([w_group[i], w_expert[i]], axis=1), LANE).astype(F32)
    b_router = _pad_cols(jnp.concatenate([b_group[i], b_expert[i]])[None], LANE).astype(F32)
    return w_router, b_router, w_gate, w_up, w_down, i


def kernel(x, p, positions, norm_mix, norm_ffn, norm_ple, norm_final, ev_w_in, ev_w_out, gdn_conv_w,
           gdn_a_log, gdn_dt_bias, gdn_norm, od_w_in, od_w_out, mla_q_norm, mla_kv_norm, mla_w_uq,
           mla_w_ukv, ssm_conv_w, ssm_conv_b, ssm_a_log, ssm_dt_bias, ssm_d_skip, ssm_norm, moe_w_group,
           moe_b_group, moe_w_expert, moe_b_expert, moe_w_gate, moe_w_up, moe_w_down, ple_w_proj,
           ple_w_gate):
    b, s, d = x.shape
    t = b * s
    tab_a, tab_i = _rope_tables(positions)
    h = x.reshape(t, d)
    moe_args = (moe_w_group, moe_b_group, moe_w_expert, moe_b_expert, moe_w_gate, moe_w_up, moe_w_down)

    w = ev_w_in[0]
    o = np_cumsum((1024, 256, 256, 512, 64, 8, 3072, 1024, 8, 8))
    w_a = _pad_cols(w[:, :o[6]], 2176).astype(BF16)
    w_b = _pad_cols(jnp.concatenate([w[:, o[6]:o[8]], w[:, o[8]:]], axis=1), 4224).astype(BF16)
    proj_a = norm_matmul(h, norm_mix[0], w_a, k=d)
    proj_b = norm_matmul(h, norm_mix[0], w_b, k=d, tn=1408)
    q, k, v, iq, ik, iw = dsa_prep(proj_a, tab_a + tab_i)
    r3 = lambda a: a.reshape(b, s, a.shape[-1])
    o_a = dsa_attention(r3(q), r3(k), r3(v), r3(iq), r3(ik), r3(iw)).reshape(t, -1)
    o_b = gdn_mixer(proj_b, gdn_conv_w[0].astype(F32), _lane_row(gdn_a_log[0], B_HEADS),
                    _lane_row(gdn_dt_bias[0], B_HEADS), gdn_norm[0].reshape(1, -1).astype(F32), batch=b)
    h = outproj_residual(o_a, o_b, ev_w_out[0].astype(BF16), h)
    h = moe_layer(h, norm_ffn[0], *_moe_weights(0, *moe_args))
    h = ple_residual(h, norm_ple[0], ple_w_gate[0].astype(BF16), p[0].reshape(t, -1),
                     ple_w_proj[0].astype(BF16), norm_final, norm_out=False)

    w = od_w_in[0]
    o = np_cumsum((512, 512, 64, 1024, 1536, 16))
    w_c = _pad_cols(w[:, :o[3]], 1152).astype(BF16)
    w_d = _pad_cols(w[:, o[3]:], 2688).astype(BF16)
    proj_c = norm_matmul(h, norm_mix[1], w_c, k=d)
    proj_d = norm_matmul(h, norm_mix[1], w_d, k=d, tn=896)
    w_uq = jnp.pad(mla_w_uq[0].reshape(C_Q_RANK, C_HEADS, C_NOPE + C_ROPE),
                   ((0, 0), (0, 0), (0, 2 * LANE - C_NOPE - C_ROPE))).reshape(C_Q_RANK, -1).astype(BF16)
    q_raw = norm_matmul(proj_c, mla_q_norm[0], w_uq, k=C_Q_RANK, xblk=0, tn=1024)
    kv = norm_matmul(proj_c, mla_kv_norm[0], mla_w_ukv[0].astype(BF16), k=C_KV_RANK, xblk=1, tn=1024,
                     out_dtype=BF16)
    q, kr = mla_prep(q_raw, proj_c, tab_i)
    o_c = mla_attention(r3(q), r3(kv), r3(kr)).reshape(t, -1)
    a_heads = -jnp.exp(ssm_a_log[0].astype(F32))
    expand = (jnp.arange(LANE)[:, None] == (jnp.arange(D_INNER)[None, :] // D_HEAD_DIM)).astype(BF16)
    o_d = ssd_mixer(proj_d, ssm_conv_w[0].astype(F32), ssm_conv_b[0].reshape(1, -1).astype(F32),
                    _lane_row(ssm_dt_bias[0], 0), _lane_row(a_heads, 0),
                    jnp.repeat(a_heads, D_HEAD_DIM)[None], jnp.repeat(ssm_d_skip[0].astype(F32), D_HEAD_DIM)[None],
                    ssm_norm[0].reshape(1, -1).astype(F32), expand, batch=b)
    h = outproj_residual(o_c, o_d, od_w_out[0].astype(BF16), h)
    h = moe_layer(h, norm_ffn[1], *_moe_weights(1, *moe_args))
    out = ple_residual(h, norm_ple[1], ple_w_gate[1].astype(BF16), p[1].reshape(t, -1),
                       ple_w_proj[1].astype(BF16), norm_final, norm_out=True)
    return out.reshape(b, s, d)


def np_cumsum(sizes):
    out, acc = [], 0
    for v in sizes:
        out.append(acc)
        acc += v
    return out
```

```python
import functools
import math

import jax
import jax.numpy as jnp
from jax import lax
from jax.experimental import pallas as pl
from jax.experimental.pallas import tpu as pltpu

F32 = jnp.float32
BF16 = jnp.bfloat16
I32 = jnp.int32

CHUNK = 64
ROPE_THETA = 10000.0
NORM_EPS = 1e-6
NEG_INF = -1e30
BISECT_STEPS = 16

A_HEADS, A_KV_HEADS, A_HEAD_DIM = 8, 2, 128
IDX_HEADS, IDX_DIM = 8, 64
IDX_SCALE = (IDX_HEADS * IDX_DIM) ** -0.5
TOPK_MAX = 256
B_HEADS, B_HEAD_DIM, CONV_WIDTH = 8, 128, 4
C_HEADS, C_Q_RANK, C_KV_RANK, C_NOPE, C_ROPE, C_V = 8, 512, 512, 128, 64, 128
D_INNER, D_HEAD_DIM, D_GROUPS, D_STATE = 1024, 64, 2, 128
D_HEADS = D_INNER // D_HEAD_DIM
N_GROUPS, EXPERTS_PER_GROUP, EXPERT_FF = 4, 8, 256
N_EXPERTS = N_GROUPS * EXPERTS_PER_GROUP

LANE = 128
VMEM_BYTES_V7X = 64 * 1024 * 1024
VMEM_LIMIT = VMEM_BYTES_V7X - 8 * 1024 * 1024


def _cparams(*sem):
    return pltpu.CompilerParams(dimension_semantics=sem, vmem_limit_bytes=VMEM_LIMIT)


def _dot(a, b):
    return jnp.dot(a, b, preferred_element_type=F32)


def _dot_nt(a, b):
    return lax.dot_general(a, b, (((1,), (1,)), ((), ())), preferred_element_type=F32)


def _bdot(a, b):
    return _dot(a.astype(BF16), b.astype(BF16))


def _bdot_nt(a, b):
    return _dot_nt(a.astype(BF16), b.astype(BF16))


def _split3(x):
    hi = x.astype(BF16)
    r1 = x - hi.astype(F32)
    mid = r1.astype(BF16)
    lo = (r1 - mid.astype(F32)).astype(BF16)
    return hi, mid, lo


def _dot01_left(m01, x):
    hi, mid, lo = _split3(x)
    return _dot(m01, hi) + _dot(m01, mid) + _dot(m01, lo)


def _dot01_right(x, m01):
    hi, mid, lo = _split3(x)
    return _dot(hi, m01) + _dot(mid, m01) + _dot(lo, m01)


def _sigmoid(x):
    return 1.0 / (1.0 + jnp.exp(-x))


def _silu(x):
    return x * _sigmoid(x)


def _softplus(x):
    return jnp.maximum(x, 0.0) + jnp.log1p(jnp.exp(-jnp.abs(x)))


def _lane_fold(x, op):
    blocks = [x[:, c * LANE:(c + 1) * LANE] for c in range(x.shape[1] // LANE)]
    while len(blocks) > 1:
        blocks = [op(blocks[i], blocks[i + 1]) for i in range(0, len(blocks), 2)]
    return blocks[0]


def _rowmax(x):
    return jnp.max(_lane_fold(x, jnp.maximum), axis=1, keepdims=True)


def _rowsum(x):
    return jnp.sum(_lane_fold(x, jnp.add), axis=1, keepdims=True)


def _rope64(x, c, slo, shi):
    return x * c + pltpu.roll(x, 96, 1) * slo + pltpu.roll(x, 32, 1) * shi


def _rope128(x, c, s):
    return x * c + pltpu.roll(x, 64, 1) * s


def _norm_matmul_kernel(x_ref, g_ref, w_ref, o_ref, xn_ref):
    @pl.when(pl.program_id(1) == 0)
    def _():
        x = x_ref[...].astype(F32)
        ms = jnp.mean(x * x, axis=-1, keepdims=True)
        xn_ref[...] = (x * lax.rsqrt(ms + NORM_EPS) * g_ref[...]).astype(BF16)

    o_ref[...] = _dot(xn_ref[...], w_ref[...]).astype(o_ref.dtype)


def norm_matmul(x, g, w, *, k, xblk=0, tm=512, tn=None, out_dtype=F32):
    t = x.shape[0]
    n = w.shape[1]
    tn = n if tn is None else tn
    tm = min(tm, t)
    return pl.pallas_call(
        _norm_matmul_kernel,
        grid=(t // tm, n // tn),
        in_specs=[pl.BlockSpec((tm, k), lambda i, j: (i, xblk)),
                  pl.BlockSpec((1, k), lambda i, j: (0, 0)),
                  pl.BlockSpec((k, tn), lambda i, j: (0, j))],
        out_specs=pl.BlockSpec((tm, tn), lambda i, j: (i, j)),
        out_shape=jax.ShapeDtypeStruct((t, n), out_dtype),
        scratch_shapes=[pltpu.VMEM((tm, k), BF16)],
        compiler_params=_cparams("parallel", "arbitrary"),
        name="norm_matmul",
    )(x, g.reshape(1, k).astype(F32), w)


def _outproj_kernel(a1_ref, a2_ref, w1_ref, w2_ref, r_ref, o_ref):
    o_ref[...] = r_ref[...] + _dot(a1_ref[...], w1_ref[...]) + _dot(a2_ref[...], w2_ref[...])


def outproj_residual(a1, a2, w, res, *, tm=512, tn=2048):
    t, kh = a1.shape
    n = w.shape[1]
    tm = min(tm, t)
    return pl.pallas_call(
        _outproj_kernel,
        grid=(t // tm, n // tn),
        in_specs=[pl.BlockSpec((tm, kh), lambda i, j: (i, 0)),
                  pl.BlockSpec((tm, kh), lambda i, j: (i, 0)),
                  pl.BlockSpec((kh, tn), lambda i, j: (0, j)),
                  pl.BlockSpec((kh, tn), lambda i, j: (1, j)),
                  pl.BlockSpec((tm, tn), lambda i, j: (i, j))],
        out_specs=pl.BlockSpec((tm, tn), lambda i, j: (i, j)),
        out_shape=jax.ShapeDtypeStruct((t, n), F32),
        compiler_params=_cparams("parallel", "arbitrary"),
        name="outproj",
    )(a1, a2, w, w, res)


def _ple_kernel(h_ref, g_ref, wg_ref, p_ref, wp_ref, gout_ref, o_ref, *, norm_out):
    xn = _rms_normed(h_ref, g_ref).astype(BF16)
    gate = _sigmoid(_dot(xn, wg_ref[...]))
    emb = _dot(p_ref[...].astype(BF16), wp_ref[...])
    y = h_ref[...] + gate * emb
    o_ref[...] = _rms(y, gout_ref[...]) if norm_out else y


def ple_residual(h, g, wg, p, wp, g_out, *, norm_out, tm=512):
    t, d = h.shape
    pd = p.shape[1]
    tm = min(tm, t)
    return pl.pallas_call(
        functools.partial(_ple_kernel, norm_out=norm_out),
        grid=(t // tm,),
        in_specs=[pl.BlockSpec((tm, d), lambda i: (i, 0)),
                  pl.BlockSpec((1, d), lambda i: (0, 0)),
                  pl.BlockSpec((d, d), lambda i: (0, 0)),
                  pl.BlockSpec((tm, pd), lambda i: (i, 0)),
                  pl.BlockSpec((pd, d), lambda i: (0, 0)),
                  pl.BlockSpec((1, d), lambda i: (0, 0))],
        out_specs=pl.BlockSpec((tm, d), lambda i: (i, 0)),
        out_shape=jax.ShapeDtypeStruct((t, d), F32),
        compiler_params=_cparams("parallel"),
        name="ple",
    )(h, g.reshape(1, d).astype(F32), wg, p, wp, g_out.reshape(1, d).astype(F32))


def _rms(x, g):
    ms = jnp.mean(x * x, axis=-1, keepdims=True)
    return x * lax.rsqrt(ms + NORM_EPS) * g


def _rms_normed(h_ref, g_ref):
    return _rms(h_ref[...], g_ref[...])


def _route(xn, wr, br):
    n = xn.shape[0]
    x_hi, x_lo, _ = _split3(xn)
    w_hi, w_lo, _ = _split3(wr)
    logits = _dot(x_hi, w_hi) + (_dot(x_hi, w_lo) + _dot(x_lo, w_hi)) + br
    lane = lax.broadcasted_iota(I32, (n, LANE), 1)
    glog = jnp.where(lane < N_GROUPS, logits, NEG_INF)
    gmax = jnp.max(glog, axis=-1, keepdims=True)
    g_sel = jnp.min(jnp.where(glog == gmax, lane, LANE), axis=-1, keepdims=True)
    p_sel = 1.0 / jnp.sum(jnp.exp(glog - gmax), axis=-1, keepdims=True)
    eidx = lane - N_GROUPS
    in_grp = (eidx >= 0) & (eidx < N_EXPERTS) & ((eidx // EXPERTS_PER_GROUP) == g_sel)
    e1 = jnp.where(in_grp, logits, NEG_INF)
    v1 = jnp.max(e1, axis=-1, keepdims=True)
    i1 = jnp.min(jnp.where(e1 == v1, eidx, LANE), axis=-1, keepdims=True)
    e2 = jnp.where(eidx == i1, NEG_INF, e1)
    v2 = jnp.max(e2, axis=-1, keepdims=True)
    i2 = jnp.min(jnp.where(e2 == v2, eidx, LANE), axis=-1, keepdims=True)
    ex = jnp.exp(v2 - v1)
    return i1, i2, p_sel / (1.0 + ex), p_sel * ex / (1.0 + ex)


def _moe_kernel(h_ref, g_ref, wr_ref, br_ref, wgu_ref, wd_ref, o_ref, xn_ref, comb_ref, *, ke):
    e = pl.program_id(1)
    tm = h_ref.shape[0]

    @pl.when(e == 0)
    def _():
        xn = _rms_normed(h_ref, g_ref)
        xn_ref[...] = xn.astype(BF16)
        o_ref[...] = h_ref[...]
        i1, i2, w1, w2 = _route(xn, wr_ref[...], br_ref[...])
        lane = lax.broadcasted_iota(I32, (tm, LANE), 1)
        comb_ref[...] = jnp.where(lane == i1, w1, 0.0) + jnp.where(lane == i2, w2, 0.0)

    xn = xn_ref[...]
    lane = lax.broadcasted_iota(I32, (tm, LANE), 1)
    comb = comb_ref[...]
    hids = []
    for j in range(ke):
        gu = _dot(xn, wgu_ref[j])
        c = jnp.sum(jnp.where(lane == e * ke + j, comb, 0.0), axis=-1, keepdims=True)
        hid = _silu(gu[:, :EXPERT_FF]) * gu[:, EXPERT_FF:] * c
        hids.append(hid.astype(BF16))
    hid = jnp.concatenate(hids, axis=1)
    wd = wd_ref[...].reshape(ke * EXPERT_FF, wd_ref.shape[2])
    o_ref[...] += _dot(hid, wd)


def moe_residual(h, g, w_router, b_router, w_gu, w_down, *, tm=512, ke=4):
    t, d = h.shape
    tm = min(tm, t)
    return pl.pallas_call(
        functools.partial(_moe_kernel, ke=ke),
        grid=(t // tm, N_EXPERTS // ke),
        in_specs=[pl.BlockSpec((tm, d), lambda i, e: (i, 0)),
                  pl.BlockSpec((1, d), lambda i, e: (0, 0)),
                  pl.BlockSpec((d, LANE), lambda i, e: (0, 0)),
                  pl.BlockSpec((1, LANE), lambda i, e: (0, 0)),
                  pl.BlockSpec((ke, d, 2 * EXPERT_FF), lambda i, e: (e, 0, 0)),
                  pl.BlockSpec((ke, EXPERT_FF, d), lambda i, e: (e, 0, 0))],
        out_specs=pl.BlockSpec((tm, d), lambda i, e: (i, 0)),
        out_shape=jax.ShapeDtypeStruct((t, d), F32),
        scratch_shapes=[pltpu.VMEM((tm, d), BF16), pltpu.VMEM((tm, LANE), F32)],
        compiler_params=_cparams("parallel", "arbitrary"),
        name="moe",
    )(h, g.reshape(1, d).astype(F32), w_router, b_router, w_gu, w_down)


MOE_SUB = 256
MOE_CAP = 48
MOE_SLOTS = N_EXPERTS * MOE_CAP


def _moe_gather_kernel(h_ref, g_ref, wr_ref, br_ref, xc_ref, meta_ref, flag_ref):
    n = h_ref.shape[0]
    xn = _rms_normed(h_ref, g_ref)
    i1, i2, w1, w2 = _route(xn, wr_ref[...], br_ref[...])
    lane = lax.broadcasted_iota(I32, (n, LANE), 1)
    onehot = jnp.where((lane == i1) | (lane == i2), 1.0, 0.0)
    earlier = (lax.broadcasted_iota(I32, (n, n), 1) < lax.broadcasted_iota(I32, (n, n), 0)).astype(BF16)
    rank = _dot(earlier, onehot.astype(BF16))
    r1 = jnp.sum(jnp.where(lane == i1, rank, 0.0), axis=-1, keepdims=True)
    r2 = jnp.sum(jnp.where(lane == i2, rank, 0.0), axis=-1, keepdims=True)
    fits = (r1 < MOE_CAP) & (r2 < MOE_CAP)
    flag_ref[...] = jnp.broadcast_to(jnp.max(jnp.where(fits, 0.0, 1.0), axis=0, keepdims=True), flag_ref.shape)
    pos1 = jnp.where(r1 < MOE_CAP, i1.astype(F32) * MOE_CAP + r1, -1.0)
    pos2 = jnp.where(r2 < MOE_CAP, i2.astype(F32) * MOE_CAP + r2, -1.0)
    meta = (jnp.where(lane == 0, pos1, 0.0) + jnp.where(lane == 1, pos2, 0.0)
            + jnp.where(lane == 2, w1, 0.0) + jnp.where(lane == 3, w2, 0.0))
    meta_ref[...] = meta
    meta_t = meta.T
    slot = lax.broadcasted_iota(I32, (MOE_SLOTS, n), 0).astype(F32)
    pick = jnp.where((slot == meta_t[0:1, :]) | (slot == meta_t[1:2, :]), 1.0, 0.0).astype(BF16)
    xc = _dot(pick, xn.astype(BF16)).astype(BF16)
    xc_ref[...] = xc.reshape(xc_ref.shape)


def _moe_ffn_kernel(x_ref, wg_ref, wu_ref, wd_ref, y_ref):
    x = x_ref[...]
    hid = _silu(_dot(x, wg_ref[...].astype(BF16))) * _dot(x, wu_ref[...].astype(BF16))
    y_ref[...] = _dot(hid.astype(BF16), wd_ref[...].astype(BF16)).astype(y_ref.dtype)


def _moe_scatter_kernel(y_ref, meta_ref, h_ref, o_ref):
    n = h_ref.shape[0]
    meta = meta_ref[...]
    slot = lax.broadcasted_iota(I32, (n, MOE_SLOTS), 1).astype(F32)
    y = y_ref[...].reshape(MOE_SLOTS, y_ref.shape[-1])
    pick = (jnp.where(slot == meta[:, 0:1], meta[:, 2:3], 0.0)
            + jnp.where(slot == meta[:, 1:2], meta[:, 3:4], 0.0)).astype(BF16)
    o_ref[...] = h_ref[...] + _dot(pick, y)


def moe_routed(h, g, w_router, b_router, w_gate, w_up, w_down, layer, *, tm=1024):
    t, d = h.shape
    nsub = t // MOE_SUB
    rows = nsub * MOE_CAP
    tm = min(tm, rows)
    xc, meta, flag = pl.pallas_call(
        _moe_gather_kernel,
        grid=(nsub,),
        in_specs=[pl.BlockSpec((MOE_SUB, d), lambda i: (i, 0)),
                  pl.BlockSpec((1, d), lambda i: (0, 0)),
                  pl.BlockSpec((d, LANE), lambda i: (0, 0)),
                  pl.BlockSpec((1, LANE), lambda i: (0, 0))],
        out_specs=[pl.BlockSpec((N_EXPERTS, None, MOE_CAP, d), lambda i: (0, i, 0, 0)),
                   pl.BlockSpec((MOE_SUB, LANE), lambda i: (i, 0)),
                   pl.BlockSpec((None, 8, LANE), lambda i: (i, 0, 0))],
        out_shape=[jax.ShapeDtypeStruct((N_EXPERTS, nsub, MOE_CAP, d), BF16),
                   jax.ShapeDtypeStruct((t, LANE), F32),
                   jax.ShapeDtypeStruct((nsub, 8, LANE), F32)],
        compiler_params=_cparams("parallel"),
        name="moe_gather",
    )(h, g.reshape(1, d).astype(F32), w_router, b_router)
    y = pl.pallas_call(
        _moe_ffn_kernel,
        grid=(N_EXPERTS, rows // tm),
        in_specs=[pl.BlockSpec((None, tm, d), lambda e, i: (e, i, 0)),
                  pl.BlockSpec((None, None, d, EXPERT_FF), lambda e, i: (layer, e, 0, 0)),
                  pl.BlockSpec((None, None, d, EXPERT_FF), lambda e, i: (layer, e, 0, 0)),
                  pl.BlockSpec((None, None, EXPERT_FF, d), lambda e, i: (layer, e, 0, 0))],
        out_specs=pl.BlockSpec((None, tm, d), lambda e, i: (e, i, 0)),
        out_shape=jax.ShapeDtypeStruct((N_EXPERTS, rows, d), BF16),
        compiler_params=_cparams("parallel", "arbitrary"),
        name="moe_ffn",
    )(xc.reshape(N_EXPERTS, rows, d), w_gate, w_up, w_down)
    out = pl.pallas_call(
        _moe_scatter_kernel,
        grid=(nsub,),
        in_specs=[pl.BlockSpec((N_EXPERTS, None, MOE_CAP, d), lambda i: (0, i, 0, 0)),
                  pl.BlockSpec((MOE_SUB, LANE), lambda i: (i, 0)),
                  pl.BlockSpec((MOE_SUB, d), lambda i: (i, 0))],
        out_specs=pl.BlockSpec((MOE_SUB, d), lambda i: (i, 0)),
        out_shape=jax.ShapeDtypeStruct((t, d), F32),
        compiler_params=_cparams("parallel"),
        name="moe_scatter",
    )(y.reshape(N_EXPERTS, nsub, MOE_CAP, d), meta, h)
    return out, jnp.max(flag)


def moe_layer(h, g, w_router, b_router, w_gate, w_up, w_down, layer):
    out, overflow = moe_routed(h, g, w_router, b_router, w_gate, w_up, w_down, layer)

    def dense():
        w_gu = jnp.concatenate([w_gate[layer], w_up[layer]], axis=-1).astype(BF16)
        return moe_residual(h, g, w_router, b_router, w_gu, w_down[layer].astype(BF16))

    return lax.cond(overflow > 0.0, dense, lambda: out)


def _dsa_prep_kernel(x_ref, ca_ref, sa_ref, ci_ref, slo_ref, shi_ref,
                     q_ref, k_ref, v_ref, iq_ref, ik_ref, iw_ref):
    ca, sa = ca_ref[...], sa_ref[...]
    ci, slo, shi = ci_ref[...], slo_ref[...], shi_ref[...]
    scale = A_HEAD_DIM ** -0.5 * math.log2(math.e)
    for h in range(A_HEADS):
        xh = x_ref[:, h * LANE:(h + 1) * LANE]
        q_ref[:, h * LANE:(h + 1) * LANE] = (_rope128(xh, ca, sa) * scale).astype(BF16)
    off = A_HEADS * A_HEAD_DIM
    for h in range(A_KV_HEADS):
        xh = x_ref[:, off + h * LANE:off + (h + 1) * LANE]
        k_ref[:, h * LANE:(h + 1) * LANE] = _rope128(xh, ca, sa).astype(BF16)
    off += A_KV_HEADS * A_HEAD_DIM
    v_ref[...] = x_ref[:, off:off + A_KV_HEADS * A_HEAD_DIM].astype(BF16)
    off += A_KV_HEADS * A_HEAD_DIM
    for h in range(IDX_HEADS * IDX_DIM // LANE):
        xh = x_ref[:, off + h * LANE:off + (h + 1) * LANE]
        iq_ref[:, h * LANE:(h + 1) * LANE] = _rope64(xh, ci, slo, shi).astype(BF16)
    off += IDX_HEADS * IDX_DIM
    misc = x_ref[:, off:off + LANE]
    ik_ref[...] = _rope64(misc, ci, slo, shi)[:, :IDX_DIM].astype(BF16)
    iw_ref[...] = misc[:, IDX_DIM:IDX_DIM + IDX_HEADS] * IDX_SCALE


def dsa_prep(proj, tabs, *, tm=512):
    t, n = proj.shape
    tm = min(tm, t)
    row = lambda w: pl.BlockSpec((tm, w), lambda i: (i, 0))
    return pl.pallas_call(
        _dsa_prep_kernel,
        grid=(t // tm,),
        in_specs=[row(n)] + [row(LANE)] * 5,
        out_specs=[row(1024), row(256), row(256), row(512), row(IDX_DIM), row(IDX_HEADS)],
        out_shape=[jax.ShapeDtypeStruct((t, 1024), BF16), jax.ShapeDtypeStruct((t, 256), BF16),
                   jax.ShapeDtypeStruct((t, 256), BF16), jax.ShapeDtypeStruct((t, 512), BF16),
                   jax.ShapeDtypeStruct((t, IDX_DIM), BF16), jax.ShapeDtypeStruct((t, IDX_HEADS), F32)],
        compiler_params=_cparams("parallel"),
        name="dsa_prep",
    )(proj, *tabs)


def _dsa_kernel(q_ref, k_ref, v_ref, iq_ref, ik_ref, iw_ref, o_ref,
                key_ref, m_ref, l_ref, acc_ref, *, tk, topk):
    qi = pl.program_id(1)
    nq = q_ref.shape[0]
    rep = A_HEADS // A_KV_HEADS
    n_tiles = ((qi + 1) * nq + tk - 1) // tk
    row = lax.broadcasted_iota(I32, (nq, tk), 0)
    col = lax.broadcasted_iota(I32, (nq, tk), 1)
    q_chunk = (qi * nq + row) // CHUNK

    def admissible(off):
        return ((off + col) // CHUNK) <= q_chunk

    iq = iq_ref[...]
    iw = iw_ref[...]

    def score_body(t, carry):
        off = pl.multiple_of(t * tk, tk)
        ikt = ik_ref[pl.ds(off, tk), :]
        s = jnp.zeros((nq, tk), F32)
        for h in range(IDX_HEADS):
            rel = _dot_nt(iq[:, h * IDX_DIM:(h + 1) * IDX_DIM], ikt)
            s = s + iw[:, h:h + 1] * jnp.maximum(rel, 0.0)
        key_ref[:, pl.ds(off, tk)] = jnp.where(admissible(off), s, NEG_INF)
        return carry

    lax.fori_loop(0, n_tiles, score_body, 0)

    def scan(fn, init):
        def body(t, acc):
            off = pl.multiple_of(t * tk, tk)
            blk = key_ref[:, pl.ds(off, tk)]
            for c in range(tk // LANE):
                acc = fn(blk[:, c * LANE:(c + 1) * LANE], acc)
            return acc
        return lax.fori_loop(0, n_tiles, body, init)

    def rowsum(x):
        return jnp.sum(x, axis=1, keepdims=True)

    def count_ge(v):
        return rowsum(scan(lambda blk, acc: acc + jnp.where(blk >= v, 1.0, 0.0), jnp.zeros((nq, LANE), F32)))

    big = jnp.full((nq, LANE), -NEG_INF, F32)
    zeros = jnp.zeros((nq, LANE), F32)
    lo_p, hi_p, n_p = scan(
        lambda blk, acc: (jnp.minimum(acc[0], jnp.where(blk > NEG_INF, blk, big)), jnp.maximum(acc[1], blk),
                          acc[2] + jnp.where(blk > NEG_INF, 1.0, 0.0)), (big, -big, zeros))
    enough = rowsum(n_p) >= topk
    lo0 = jnp.where(enough, jnp.min(lo_p, axis=1, keepdims=True), NEG_INF)
    hi0 = jnp.where(enough, jnp.max(hi_p, axis=1, keepdims=True), NEG_INF)

    def bisect_body(i, bracket):
        lo, hi = bracket
        mid = lo + 0.5 * (hi - lo)
        ge = count_ge(mid) >= topk
        return jnp.where(ge, mid, lo), jnp.where(ge, hi, mid)

    _, hi = lax.fori_loop(0, BISECT_STEPS, bisect_body, (lo0, hi0))

    def walk_cond(state):
        return state[1] > 0

    def walk_body(state):
        v, _ = state
        zero = jnp.zeros((nq, LANE), F32)
        cnt_p, nxt_p = scan(lambda blk, acc: (acc[0] + jnp.where(blk >= v, 1.0, 0.0),
                                              jnp.maximum(acc[1], jnp.where(blk < v, blk, -big))),
                            (zero, -big))
        found = rowsum(cnt_p) >= topk
        v = jnp.where(found, v, jnp.max(nxt_p, axis=1, keepdims=True))
        return v, jnp.sum(jnp.where(found, 0, 1))

    hi_val = jnp.max(scan(lambda blk, acc: jnp.maximum(acc, jnp.where(blk <= hi, blk, -big)), -big),
                     axis=1, keepdims=True)
    thr, _ = lax.while_loop(walk_cond, walk_body, (hi_val, jnp.int32(1)))
    n_gt = rowsum(scan(lambda blk, acc: acc + jnp.where(blk > thr, 1.0, 0.0), jnp.zeros((nq, LANE), F32)))
    need = topk - n_gt

    m_ref[...] = jnp.full(m_ref.shape, NEG_INF, F32)
    l_ref[...] = jnp.zeros(l_ref.shape, F32)
    acc_ref[...] = jnp.zeros(acc_ref.shape, F32)
    upper = (lax.broadcasted_iota(I32, (LANE, LANE), 0)
             < lax.broadcasted_iota(I32, (LANE, LANE), 1)).astype(BF16)
    qg = [jnp.concatenate([q_ref[:, (g * rep + r) * LANE:(g * rep + r + 1) * LANE]
                           for r in range(rep)], axis=0) for g in range(A_KV_HEADS)]

    def attn_body(t, run):
        off = pl.multiple_of(t * tk, tk)
        keyt = key_ref[:, pl.ds(off, tk)]
        adm = admissible(off)
        sels = []
        for c in range(tk // LANE):
            kc = keyt[:, c * LANE:(c + 1) * LANE]
            eqf = jnp.where(kc == thr, 1.0, 0.0)
            before = _dot(eqf.astype(BF16), upper) + run
            take = jnp.where(kc > thr, 1.0, jnp.where(before < need, eqf, 0.0))
            sels.append(take)
            run = run + jnp.sum(eqf, axis=1, keepdims=True)
        sel = jnp.where(adm, jnp.concatenate(sels, axis=1), 0.0) > 0.5
        sel_r = jnp.concatenate([sel] * rep, axis=0)
        groups = range(A_KV_HEADS)
        ss = [jnp.where(sel_r, _dot_nt(qg[g], k_ref[pl.ds(off, tk), g * LANE:(g + 1) * LANE]), NEG_INF)
              for g in groups]
        m_old = [m_ref[g] for g in groups]
        m_new = [jnp.maximum(m_old[g], _rowmax(ss[g])) for g in groups]
        ps = [jnp.exp2(ss[g] - m_new[g]) for g in groups]
        pv = [_dot(ps[g].astype(BF16), v_ref[pl.ds(off, tk), g * LANE:(g + 1) * LANE]) for g in groups]
        for g in groups:
            alpha = jnp.exp2(m_old[g] - m_new[g])
            l_ref[g] = alpha * l_ref[g] + _rowsum(ps[g])
            acc_ref[g] = alpha * acc_ref[g] + pv[g]
            m_ref[g] = m_new[g]
        return run

    lax.fori_loop(0, n_tiles, attn_body, jnp.zeros((nq, 1), F32))
    for g in range(A_KV_HEADS):
        out = acc_ref[g] / l_ref[g]
        for r in range(rep):
            h = g * rep + r
            o_ref[:, h * LANE:(h + 1) * LANE] = out[r * nq:(r + 1) * nq].astype(o_ref.dtype)


def dsa_attention(q, k, v, iq, ik, iw, *, nq=128, tk=1024):
    b, s, _ = q.shape
    tk = min(tk, s)
    topk = min(TOPK_MAX, s // 4)
    rep = A_HEADS // A_KV_HEADS
    qspec = lambda w: pl.BlockSpec((None, nq, w), lambda bi, qi: (bi, qi, 0))
    kspec = lambda w: pl.BlockSpec((None, s, w), lambda bi, qi: (bi, 0, 0))
    return pl.pallas_call(
        functools.partial(_dsa_kernel, tk=tk, topk=topk),
        grid=(b, s // nq),
        in_specs=[qspec(1024), kspec(256), kspec(256), qspec(512), kspec(IDX_DIM), qspec(IDX_HEADS)],
        out_specs=qspec(1024),
        out_shape=jax.ShapeDtypeStruct((b, s, 1024), BF16),
        scratch_shapes=[pltpu.VMEM((nq, s), F32),
                        pltpu.VMEM((A_KV_HEADS, rep * nq, 1), F32),
                        pltpu.VMEM((A_KV_HEADS, rep * nq, 1), F32),
                        pltpu.VMEM((A_KV_HEADS, rep * nq, LANE), F32)],
        compiler_params=_cparams("parallel", "arbitrary"),
        name="dsa_attention",
    )(q, k, v, iq, ik, iw)


def _causal_conv(x, tail_ref, xp_ref, w):
    r = x.shape[0]
    width = w.shape[0]
    xp_ref[0:8, :] = tail_ref[...]
    xp_ref[8:8 + r, :] = x
    tail_ref[...] = x[r - 8:r, :]
    acc = x * w[width - 1:width, :]
    for j in range(width - 1):
        acc = acc + xp_ref[pl.ds(8 - (width - 1) + j, r), :] * w[j:j + 1, :]
    return acc


def _unit_lower_inverse(mats):
    n = mats[0].shape[0]
    eye = (lax.broadcasted_iota(I32, (n, n), 0) == lax.broadcasted_iota(I32, (n, n), 1)).astype(F32)
    ps = [-a for a in mats]
    ts = [eye + p for p in ps]
    for _ in range(int(math.ceil(math.log2(n))) - 1):
        ps = [_bdot(p, p) for p in ps]
        ts = [t + _bdot(t, p) for t, p in zip(ts, ps)]
    return ts


def _gdn_kernel(qkv_ref, z_ref, misc_ref, convw_ref, alog_ref, dtb_ref, nw_ref, o_ref,
                tail_ref, xp_ref, q_s, k_s, v_s, beta_s, g_s, state_ref):
    nb, rows = qkv_ref.shape[0], qkv_ref.shape[1]
    hd = B_HEAD_DIM
    nh = B_HEADS

    @pl.when(pl.program_id(1) == 0)
    def _():
        tail_ref[...] = jnp.zeros(tail_ref.shape, F32)
        state_ref[...] = jnp.zeros(state_ref.shape, F32)

    for b in range(nb):
        y = _silu(_causal_conv(qkv_ref[b], tail_ref.at[b], xp_ref.at[b], convw_ref[...]))
        for h in range(nh):
            qh = y[:, h * hd:(h + 1) * hd]
            kh = y[:, (nh + h) * hd:(nh + h + 1) * hd]
            q_s[b, :, h * hd:(h + 1) * hd] = (qh * lax.rsqrt(jnp.sum(qh * qh, -1, keepdims=True) + 1e-6)
                                              * hd ** -0.5)
            k_s[b, :, h * hd:(h + 1) * hd] = kh * lax.rsqrt(jnp.sum(kh * kh, -1, keepdims=True) + 1e-6)
        v_s[b] = y[:, 2 * nh * hd:]
        misc = misc_ref[b]
        beta_s[b] = _sigmoid(misc)
        g_s[b] = -jnp.exp(alog_ref[...]) * _softplus(misc + dtb_ref[...])

    ri = lax.broadcasted_iota(I32, (CHUNK, CHUNK), 0)
    ci = lax.broadcasted_iota(I32, (CHUNK, CHUNK), 1)
    incl = ri >= ci
    strict = ri > ci
    lower01 = incl.astype(BF16)
    nw = nw_ref[...]

    def chunk_body(c, carry):
        r0 = pl.multiple_of(c * CHUNK, CHUNK)
        rs = pl.ds(r0, CHUNK)
        gc = [_dot01_left(lower01, g_s[b, rs, :]) for b in range(nb)]
        gct = [x.T for x in gc]
        beta = [beta_s[b, rs, :] for b in range(nb)]
        eg = [jnp.exp(x) for x in gc]
        e_last = [jnp.exp(x[CHUNK - 1:CHUNK, :]) for x in gc]
        e_rem = [jnp.exp(x[CHUNK - 1:CHUNK, :] - x) for x in gc]
        items = [(b, h) for b in range(nb) for h in range(nh)]
        n = range(len(items))
        hs = [slice(h * hd, (h + 1) * hd) for _, h in items]
        col = lambda x, i: x[items[i][0]][:, nh + items[i][1]:nh + items[i][1] + 1]
        qh = [q_s[items[i][0], rs, hs[i]] for i in n]
        kh = [k_s[items[i][0], rs, hs[i]] for i in n]
        bcol = [beta[b][:, h:h + 1] for b, h in items]
        kb = [kh[i] * bcol[i] for i in n]
        kk = [_bdot_nt(kb[i], kh[i]) for i in n]
        qk = [_bdot_nt(qh[i], kh[i]) for i in n]
        st = [state_ref[i] for i in n]
        q_st = [_bdot(qh[i] * col(eg, i), st[i]) for i in n]
        decay = []
        for i, (b, h) in enumerate(items):
            diff = col(gc, i) - gct[b][nh + h:nh + h + 1, :]
            decay.append(jnp.where(incl, jnp.exp(jnp.where(incl, diff, 0.0)), 0.0))
        t_mat = _unit_lower_inverse([jnp.where(strict, kk[i] * decay[i], 0.0) for i in n])
        uw = [_bdot(t_mat[i], jnp.concatenate([v_s[items[i][0], rs, hs[i]] * bcol[i], kb[i] * col(eg, i)],
                                              axis=1)) for i in n]
        w_st = [_bdot(uw[i][:, hd:], st[i]) for i in n]
        v_new = [uw[i][:, :hd] - w_st[i] for i in n]
        o = [q_st[i] + _bdot(qk[i] * decay[i], v_new[i]) for i in n]
        upd = [_bdot((kh[i] * col(e_rem, i)).T, v_new[i]) for i in n]
        for i, (b, h) in enumerate(items):
            state_ref[i] = st[i] * col(e_last, i) + upd[i]
            on = o[i] * lax.rsqrt(jnp.mean(o[i] * o[i], -1, keepdims=True) + NORM_EPS) * nw
            o_ref[b, rs, hs[i]] = (on * _silu(z_ref[b, rs, hs[i]])).astype(o_ref.dtype)
        return carry

    lax.fori_loop(0, rows // CHUNK, chunk_body, 0)


def gdn_mixer(proj, conv_w, alog_row, dtb_row, norm_w, *, batch, rows=256, nb=2):
    t = proj.shape[0]
    s = t // batch
    rows = min(rows, s)
    nb = math.gcd(nb, batch)
    nqkv = 3 * B_HEADS * B_HEAD_DIM
    nz = B_HEADS * B_HEAD_DIM
    proj = proj.reshape(batch, s, proj.shape[1])
    const = lambda shape: pl.BlockSpec(shape, lambda b, r: (0, 0))
    out = pl.pallas_call(
        _gdn_kernel,
        grid=(batch // nb, s // rows),
        in_specs=[pl.BlockSpec((nb, rows, nqkv), lambda b, r: (b, r, 0)),
                  pl.BlockSpec((nb, rows, nz), lambda b, r: (b, r, nqkv // nz)),
                  pl.BlockSpec((nb, rows, LANE), lambda b, r: (b, r, (nqkv + nz) // LANE)),
                  const((CONV_WIDTH, nqkv)), const((1, LANE)), const((1, LANE)), const((1, LANE))],
        out_specs=pl.BlockSpec((nb, rows, nz), lambda b, r: (b, r, 0)),
        out_shape=jax.ShapeDtypeStruct((batch, s, nz), BF16),
        scratch_shapes=[pltpu.VMEM((nb, 8, nqkv), F32), pltpu.VMEM((nb, rows + 8, nqkv), F32),
                        pltpu.VMEM((nb, rows, nz), F32), pltpu.VMEM((nb, rows, nz), F32),
                        pltpu.VMEM((nb, rows, nz), F32),
                        pltpu.VMEM((nb, rows, LANE), F32), pltpu.VMEM((nb, rows, LANE), F32),
                        pltpu.VMEM((nb * B_HEADS, B_HEAD_DIM, B_HEAD_DIM), F32)],
        compiler_params=_cparams("parallel", "arbitrary"),
        name="gdn",
    )(proj, proj, proj, conv_w, alog_row, dtb_row, norm_w)
    return out.reshape(t, nz)


def _mla_prep_kernel(q_ref, kr_ref, ci_ref, slo_ref, shi_ref, qo_ref, kro_ref):
    ci, slo, shi = ci_ref[...], slo_ref[...], shi_ref[...]
    scale = (C_NOPE + C_ROPE) ** -0.5 * math.log2(math.e)
    for h in range(C_HEADS):
        base = h * 2 * LANE
        qo_ref[:, base:base + LANE] = (q_ref[:, base:base + LANE] * scale).astype(BF16)
        qo_ref[:, base + LANE:base + 2 * LANE] = (
            _rope64(q_ref[:, base + LANE:base + 2 * LANE], ci, slo, shi) * scale).astype(BF16)
    kro_ref[...] = _rope64(kr_ref[...], ci, slo, shi).astype(BF16)


def mla_prep(q_raw, proj_c, tabs, *, tm=512):
    t, n = q_raw.shape
    tm = min(tm, t)
    row = lambda w: pl.BlockSpec((tm, w), lambda i: (i, 0))
    return pl.pallas_call(
        _mla_prep_kernel,
        grid=(t // tm,),
        in_specs=[row(n), pl.BlockSpec((tm, LANE), lambda i: (i, (C_Q_RANK + C_KV_RANK) // LANE)),
                  row(LANE), row(LANE), row(LANE)],
        out_specs=[row(n), row(LANE)],
        out_shape=[jax.ShapeDtypeStruct((t, n), BF16), jax.ShapeDtypeStruct((t, LANE), BF16)],
        compiler_params=_cparams("parallel"),
        name="mla_prep",
    )(q_raw, proj_c, *tabs)


def _mla_kernel(q_ref, kv_ref, kr_ref, o_ref, m_ref, l_ref, acc_ref, *, hb):
    qi = pl.program_id(2)
    tq = q_ref.shape[0]
    heads = range(hb)
    row = lax.broadcasted_iota(I32, (tq, tq), 0)
    col = lax.broadcasted_iota(I32, (tq, tq), 1)
    diag_mask = (col // CHUNK) <= (row // CHUNK)
    m_ref[...] = jnp.full(m_ref.shape, NEG_INF, F32)
    l_ref[...] = jnp.zeros(l_ref.shape, F32)
    acc_ref[...] = jnp.zeros(acc_ref.shape, F32)
    qs = [q_ref[:, h * 2 * LANE:(h + 1) * 2 * LANE] for h in heads]

    def step(off, on_diagonal):
        rows = pl.ds(off, tq)
        kr = kr_ref[rows, :]
        def scores(h):
            s = _dot_nt(qs[h], jnp.concatenate([kv_ref[rows, h * 2 * LANE:h * 2 * LANE + LANE], kr], axis=1))
            return jnp.where(diag_mask, s, NEG_INF) if on_diagonal else s

        def update(h, s):
            m_old = m_ref[h]
            m_new = jnp.maximum(m_old, _rowmax(s))
            p = jnp.exp2(s - m_new)
            pv = _dot(p.astype(BF16), kv_ref[rows, h * 2 * LANE + LANE:(h + 1) * 2 * LANE])
            alpha = jnp.exp2(m_old - m_new)
            l_ref[h] = alpha * l_ref[h] + _rowsum(p)
            acc_ref[h] = alpha * acc_ref[h] + pv
            m_ref[h] = m_new

        s_next = scores(0)
        for h in heads:
            s_cur = s_next
            if h + 1 < hb:
                s_next = scores(h + 1)
            update(h, s_cur)

    def body(t, carry):
        step(pl.multiple_of(t * tq, tq), False)
        return carry

    lax.fori_loop(0, qi, body, 0)
    step(pl.multiple_of(qi * tq, tq), True)
    for h in heads:
        o_ref[:, h * LANE:(h + 1) * LANE] = (acc_ref[h] / l_ref[h]).astype(o_ref.dtype)


def mla_attention(q, kv, kr, *, tq=1024, hb=2):
    b, s, _ = q.shape
    tq = min(tq, s)
    return pl.pallas_call(
        functools.partial(_mla_kernel, hb=hb),
        grid=(b, C_HEADS // hb, s // tq),
        in_specs=[pl.BlockSpec((None, tq, hb * 2 * LANE), lambda bi, h, qi: (bi, qi, h)),
                  pl.BlockSpec((None, s, hb * 2 * LANE), lambda bi, h, qi: (bi, 0, h)),
                  pl.BlockSpec((None, s, LANE), lambda bi, h, qi: (bi, 0, 0))],
        out_specs=pl.BlockSpec((None, tq, hb * LANE), lambda bi, h, qi: (bi, qi, h)),
        out_shape=jax.ShapeDtypeStruct((b, s, C_HEADS * C_V), BF16),
        scratch_shapes=[pltpu.VMEM((hb, tq, 1), F32), pltpu.VMEM((hb, tq, 1), F32),
                        pltpu.VMEM((hb, tq, LANE), F32)],
        compiler_params=_cparams("parallel", "parallel", "arbitrary"),
        name="mla_attention",
    )(q, kv, kr)


def _ssd_kernel(zx_ref, misc_ref, convw_ref, convb_ref, dtb_ref, a_ref, aexp_ref, dskip_ref,
                nw_ref, expand_ref, o_ref,
                tail_ref, xp_ref, xs_s, xdt_s, bm_s, cm_s, la_s, laexp_s, y_s, state_ref):
    rows = zx_ref.shape[0]
    pd = D_HEAD_DIM
    gw = D_INNER // D_GROUPS
    pairs_per_group = gw // LANE

    @pl.when(pl.program_id(1) == 0)
    def _():
        tail_ref[...] = jnp.zeros(tail_ref.shape, F32)
        state_ref[...] = jnp.zeros(state_ref.shape, F32)

    y = _silu(_causal_conv(zx_ref[:, D_INNER:], tail_ref, xp_ref, convw_ref[...]) + convb_ref[...])
    xs = y[:, :D_INNER]
    xs_s[...] = xs
    bm_s[...] = y[:, D_INNER:D_INNER + D_GROUPS * D_STATE]
    cm_s[...] = y[:, D_INNER + D_GROUPS * D_STATE:]
    dt = _softplus(misc_ref[...] + dtb_ref[...])
    dt_exp = _dot01_right(dt, expand_ref[...])
    xdt_s[...] = xs * dt_exp
    la_s[...] = dt * a_ref[...]
    laexp_s[...] = dt_exp * aexp_ref[...]

    ri = lax.broadcasted_iota(I32, (CHUNK, CHUNK), 0)
    ci = lax.broadcasted_iota(I32, (CHUNK, CHUNK), 1)
    incl = ri >= ci
    lower01 = incl.astype(BF16)
    lane_lo = lax.broadcasted_iota(I32, (CHUNK, LANE), 1) < pd

    def chunk_body(c, carry):
        r0 = pl.multiple_of(c * CHUNK, CHUNK)
        rs = pl.ds(r0, CHUNK)
        cum = _dot01_left(lower01, la_s[rs, :])
        cumt = cum.T
        cum_e = _dot01_left(lower01, laexp_s[rs, :])
        ecum = jnp.exp(cum_e)
        last = cum_e[CHUNK - 1:CHUNK, :]
        dec_st = jnp.exp(last - cum_e)
        cdec = jnp.exp(last)
        for g in range(D_GROUPS):
            bm = bm_s[rs, g * D_STATE:(g + 1) * D_STATE]
            cm = cm_s[rs, g * D_STATE:(g + 1) * D_STATE]
            cb = _bdot_nt(cm, bm)
            bmt = bm.T.astype(BF16)
            gs = slice(g * gw, (g + 1) * gw)
            y_off = _bdot(cm, state_ref[:, gs]) * ecum[:, gs]
            for pp in range(pairs_per_group):
                pidx = g * pairs_per_group + pp
                ps = slice(pidx * LANE, (pidx + 1) * LANE)
                xp = xdt_s[rs, ps]
                ys = []
                for hh in range(2):
                    h = 2 * pidx + hh
                    diff = cum[:, h:h + 1] - cumt[h:h + 1, :]
                    lmat = jnp.where(incl, jnp.exp(jnp.where(incl, diff, 0.0)), 0.0)
                    ys.append(_bdot(cb * lmat, xp))
                y_diag = jnp.where(lane_lo, ys[0], ys[1])
                y_s[rs, ps] = (y_diag + y_off[:, pp * LANE:(pp + 1) * LANE]
                               + dskip_ref[:, ps] * xs_s[rs, ps])
                s_new = _dot(bmt, (xp * dec_st[:, ps]).astype(BF16))
                state_ref[:, ps] = state_ref[:, ps] * cdec[:, ps] + s_new
        return carry

    lax.fori_loop(0, rows // CHUNK, chunk_body, 0)
    yv = y_s[...] * _silu(zx_ref[:, :D_INNER])
    for g in range(D_GROUPS):
        gs = slice(g * gw, (g + 1) * gw)
        yg = yv[:, gs]
        ms = jnp.mean(yg * yg, axis=-1, keepdims=True)
        o_ref[:, gs] = (yg * lax.rsqrt(ms + NORM_EPS) * nw_ref[:, gs]).astype(o_ref.dtype)


def ssd_mixer(proj, conv_w, conv_b, dtb_row, a_row, a_exp, dskip_exp, norm_w, expand, *, batch, rows=256):
    t = proj.shape[0]
    s = t // batch
    rows = min(rows, s)
    nr = s // rows
    nx = D_INNER + 2 * D_GROUPS * D_STATE
    const = lambda shape: pl.BlockSpec(shape, lambda b, r: (0, 0))
    return pl.pallas_call(
        _ssd_kernel,
        grid=(batch, nr),
        in_specs=[pl.BlockSpec((rows, D_INNER + nx), lambda b, r: (b * nr + r, 0)),
                  pl.BlockSpec((rows, LANE), lambda b, r: (b * nr + r, (D_INNER + nx) // LANE)),
                  const((CONV_WIDTH, nx)), const((1, nx)), const((1, LANE)), const((1, LANE)),
                  const((1, D_INNER)), const((1, D_INNER)), const((1, D_INNER)), const((LANE, D_INNER))],
        out_specs=pl.BlockSpec((rows, D_INNER), lambda b, r: (b * nr + r, 0)),
        out_shape=jax.ShapeDtypeStruct((t, D_INNER), BF16),
        scratch_shapes=[pltpu.VMEM((8, nx), F32), pltpu.VMEM((rows + 8, nx), F32),
                        pltpu.VMEM((rows, D_INNER), F32), pltpu.VMEM((rows, D_INNER), F32),
                        pltpu.VMEM((rows, D_GROUPS * D_STATE), F32), pltpu.VMEM((rows, D_GROUPS * D_STATE), F32),
                        pltpu.VMEM((rows, LANE), F32), pltpu.VMEM((rows, D_INNER), F32),
                        pltpu.VMEM((rows, D_INNER), F32),
                        pltpu.VMEM((D_STATE, D_INNER), F32)],
        compiler_params=_cparams("parallel", "arbitrary"),
        name="ssd",
    )(proj, proj, conv_w, conv_b, dtb_row, a_row, a_exp, dskip_exp, norm_w, expand)


def _rope_tables(positions):
    pos = positions.reshape(-1).astype(F32)[:, None]

    def cs(dim):
        inv = jnp.power(ROPE_THETA, -jnp.arange(0, dim, 2, dtype=F32) / dim)
        ang = pos * inv
        return jnp.cos(ang), jnp.sin(ang)

    c, s = cs(A_HEAD_DIM)
    tab_a = (jnp.concatenate([c, c], -1), jnp.concatenate([-s, s], -1))
    c, s = cs(IDX_DIM)
    z = jnp.zeros_like(s)
    tab_i = (jnp.concatenate([c, c, c, c], -1), jnp.concatenate([-s, z, -s, z], -1),
             jnp.concatenate([z, s, z, s], -1))
    return tab_a, tab_i


def _pad_cols(w, n):
    return jnp.pad(w, ((0, 0), (0, n - w.shape[1])))


def _lane_row(v, start):
    return jnp.zeros((1, LANE), F32).at[0, start:start + v.shape[0]].set(v.astype(F32))


def _moe_weights(i, w_group, b_group, w_expert, b_expert, w_gate, w_up, w_down):
    w_router = _pad_cols(jnp.concatenate([w_group[i], w_expert[i]], axis=1), LANE).astype(F32)
    b_router = _pad_cols(jnp.concatenate([b_group[i], b_expert[i]])[None], LANE).astype(F32)
    return w_router, b_router, w_gate, w_up, w_down, i


def kernel(x, p, positions, norm_mix, norm_ffn, norm_ple, norm_final, ev_w_in, ev_w_out, gdn_conv_w,
           gdn_a_log, gdn_dt_bias, gdn_norm, od_w_in, od_w_out, mla_q_norm, mla_kv_norm, mla_w_uq,
           mla_w_ukv, ssm_conv_w, ssm_conv_b, ssm_a_log, ssm_dt_bias, ssm_d_skip, ssm_norm, moe_w_group,
           moe_b_group, moe_w_expert, moe_b_expert, moe_w_gate, moe_w_up, moe_w_down, ple_w_proj,
           ple_w_gate):
    b, s, d = x.shape
    t = b * s
    tab_a, tab_i = _rope_tables(positions)
    h = x.reshape(t, d)
    moe_args = (moe_w_group, moe_b_group, moe_w_expert, moe_b_expert, moe_w_gate, moe_w_up, moe_w_down)

    w = ev_w_in[0]
    o = np_cumsum((1024, 256, 256, 512, 64, 8, 3072, 1024, 8, 8))
    w_a = _pad_cols(w[:, :o[6]], 2176).astype(BF16)
    w_b = _pad_cols(jnp.concatenate([w[:, o[6]:o[8]], w[:, o[8]:]], axis=1), 4224).astype(BF16)
    proj_a = norm_matmul(h, norm_mix[0], w_a, k=d)
    proj_b = norm_matmul(h, norm_mix[0], w_b, k=d, tn=1408)
    q, k, v, iq, ik, iw = dsa_prep(proj_a, tab_a + tab_i)
    r3 = lambda a: a.reshape(b, s, a.shape[-1])
    o_a = dsa_attention(r3(q), r3(k), r3(v), r3(iq), r3(ik), r3(iw)).reshape(t, -1)
    o_b = gdn_mixer(proj_b, gdn_conv_w[0].astype(F32), _lane_row(gdn_a_log[0], B_HEADS),
                    _lane_row(gdn_dt_bias[0], B_HEADS), gdn_norm[0].reshape(1, -1).astype(F32), batch=b)
    h = outproj_residual(o_a, o_b, ev_w_out[0].astype(BF16), h)
    h = moe_layer(h, norm_ffn[0], *_moe_weights(0, *moe_args))
    h = ple_residual(h, norm_ple[0], ple_w_gate[0].astype(BF16), p[0].reshape(t, -1),
                     ple_w_proj[0].astype(BF16), norm_final, norm_out=False)

    w = od_w_in[0]
    o = np_cumsum((512, 512, 64, 1024, 1536, 16))
    w_c = _pad_cols(w[:, :o[3]], 1152).astype(BF16)
    w_d = _pad_cols(w[:, o[3]:], 2688).astype(BF16)
    proj_c = norm_matmul(h, norm_mix[1], w_c, k=d)
    proj_d = norm_matmul(h, norm_mix[1], w_d, k=d, tn=896)
    w_uq = jnp.pad(mla_w_uq[0].reshape(C_Q_RANK, C_HEADS, C_NOPE + C_ROPE),
                   ((0, 0), (0, 0), (0, 2 * LANE - C_NOPE - C_ROPE))).reshape(C_Q_RANK, -1).astype(BF16)
    q_raw = norm_matmul(proj_c, mla_q_norm[0], w_uq, k=C_Q_RANK, xblk=0, tn=1024)
    kv = norm_matmul(proj_c, mla_kv_norm[0], mla_w_ukv[0].astype(BF16), k=C_KV_RANK, xblk=1, tn=1024,
                     out_dtype=BF16)
    q, kr = mla_prep(q_raw, proj_c, tab_i)
    o_c = mla_attention(r3(q), r3(kv), r3(kr)).reshape(t, -1)
    a_heads = -jnp.exp(ssm_a_log[0].astype(F32))
    expand = (jnp.arange(LANE)[:, None] == (jnp.arange(D_INNER)[None, :] // D_HEAD_DIM)).astype(BF16)
    o_d = ssd_mixer(proj_d, ssm_conv_w[0].astype(F32), ssm_conv_b[0].reshape(1, -1).astype(F32),
                    _lane_row(ssm_dt_bias[0], 0), _lane_row(a_heads, 0),
                    jnp.repeat(a_heads, D_HEAD_DIM)[None], jnp.repeat(ssm_d_skip[0].astype(F32), D_HEAD_DIM)[None],
                    ssm_norm[0].reshape(1, -1).astype(F32), expand, batch=b)
    h = outproj_residual(o_c, o_d, od_w_out[0].astype(BF16), h)
    h = moe_layer(h, norm_ffn[1], *_moe_weights(1, *moe_args))
    out = ple_residual(h, norm_ple[1], ple_w_gate[1].astype(BF16), p[1].reshape(t, -1),
                       ple_w_proj[1].astype(BF16), norm_final, norm_out=True)
    return out.reshape(b, s, d)


def np_cumsum(sizes):
    out, acc = [], 0
    for v in sizes:
        out.append(acc)
        acc += v
    return out
```

```python
import functools
import math

import jax
import jax.numpy as jnp
from jax import lax
from jax.experimental import pallas as pl
from jax.experimental.pallas import tpu as pltpu

F32 = jnp.float32
BF16 = jnp.bfloat16
I32 = jnp.int32

CHUNK = 64
ROPE_THETA = 10000.0
NORM_EPS = 1e-6
NEG_INF = -1e30
BISECT_STEPS = 16

A_HEADS, A_KV_HEADS, A_HEAD_DIM = 8, 2, 128
IDX_HEADS, IDX_DIM = 8, 64
IDX_SCALE = (IDX_HEADS * IDX_DIM) ** -0.5
TOPK_MAX = 256
B_HEADS, B_HEAD_DIM, CONV_WIDTH = 8, 128, 4
C_HEADS, C_Q_RANK, C_KV_RANK, C_NOPE, C_ROPE, C_V = 8, 512, 512, 128, 64, 128
D_INNER, D_HEAD_DIM, D_GROUPS, D_STATE = 1024, 64, 2, 128
D_HEADS = D_INNER // D_HEAD_DIM
N_GROUPS, EXPERTS_PER_GROUP, EXPERT_FF = 4, 8, 256
N_EXPERTS = N_GROUPS * EXPERTS_PER_GROUP

LANE = 128
VMEM_BYTES_V7X = 64 * 1024 * 1024
VMEM_LIMIT = VMEM_BYTES_V7X - 8 * 1024 * 1024


def _cparams(*sem):
    return pltpu.CompilerParams(dimension_semantics=sem, vmem_limit_bytes=VMEM_LIMIT)


def _dot(a, b):
    return jnp.dot(a, b, preferred_element_type=F32)


def _dot_nt(a, b):
    return lax.dot_general(a, b, (((1,), (1,)), ((), ())), preferred_element_type=F32)


def _bdot(a, b):
    return _dot(a.astype(BF16), b.astype(BF16))


def _bdot_nt(a, b):
    return _dot_nt(a.astype(BF16), b.astype(BF16))


def _split3(x):
    hi = x.astype(BF16)
    r1 = x - hi.astype(F32)
    mid = r1.astype(BF16)
    lo = (r1 - mid.astype(F32)).astype(BF16)
    return hi, mid, lo


def _dot01_left(m01, x):
    hi, mid, lo = _split3(x)
    return _dot(m01, hi) + _dot(m01, mid) + _dot(m01, lo)


def _dot01_right(x, m01):
    hi, mid, lo = _split3(x)
    return _dot(hi, m01) + _dot(mid, m01) + _dot(lo, m01)


def _sigmoid(x):
    return 1.0 / (1.0 + jnp.exp(-x))


def _silu(x):
    return x * _sigmoid(x)


def _softplus(x):
    return jnp.maximum(x, 0.0) + jnp.log1p(jnp.exp(-jnp.abs(x)))


def _lane_fold(x, op):
    blocks = [x[:, c * LANE:(c + 1) * LANE] for c in range(x.shape[1] // LANE)]
    while len(blocks) > 1:
        blocks = [op(blocks[i], blocks[i + 1]) for i in range(0, len(blocks), 2)]
    return blocks[0]


def _rowmax(x):
    return jnp.max(_lane_fold(x, jnp.maximum), axis=1, keepdims=True)


def _rowsum(x):
    return jnp.sum(_lane_fold(x, jnp.add), axis=1, keepdims=True)


def _rope64(x, c, slo, shi):
    return x * c + pltpu.roll(x, 96, 1) * slo + pltpu.roll(x, 32, 1) * shi


def _rope128(x, c, s):
    return x * c + pltpu.roll(x, 64, 1) * s


def _norm_matmul_kernel(x_ref, g_ref, w_ref, o_ref, xn_ref):
    @pl.when(pl.program_id(1) == 0)
    def _():
        x = x_ref[...].astype(F32)
        ms = jnp.mean(x * x, axis=-1, keepdims=True)
        xn_ref[...] = (x * lax.rsqrt(ms + NORM_EPS) * g_ref[...]).astype(BF16)

    o_ref[...] = _dot(xn_ref[...], w_ref[...]).astype(o_ref.dtype)


def norm_matmul(x, g, w, *, k, xblk=0, tm=512, tn=None, out_dtype=F32):
    t = x.shape[0]
    n = w.shape[1]
    tn = n if tn is None else tn
    tm = min(tm, t)
    return pl.pallas_call(
        _norm_matmul_kernel,
        grid=(t // tm, n // tn),
        in_specs=[pl.BlockSpec((tm, k), lambda i, j: (i, xblk)),
                  pl.BlockSpec((1, k), lambda i, j: (0, 0)),
                  pl.BlockSpec((k, tn), lambda i, j: (0, j))],
        out_specs=pl.BlockSpec((tm, tn), lambda i, j: (i, j)),
        out_shape=jax.ShapeDtypeStruct((t, n), out_dtype),
        scratch_shapes=[pltpu.VMEM((tm, k), BF16)],
        compiler_params=_cparams("parallel", "arbitrary"),
        name="norm_matmul",
    )(x, g.reshape(1, k).astype(F32), w)


def _outproj_kernel(a1_ref, a2_ref, w1_ref, w2_ref, r_ref, o_ref):
    o_ref[...] = r_ref[...] + _dot(a1_ref[...], w1_ref[...]) + _dot(a2_ref[...], w2_ref[...])


def outproj_residual(a1, a2, w, res, *, tm=512, tn=2048):
    t, kh = a1.shape
    n = w.shape[1]
    tm = min(tm, t)
    return pl.pallas_call(
        _outproj_kernel,
        grid=(t // tm, n // tn),
        in_specs=[pl.BlockSpec((tm, kh), lambda i, j: (i, 0)),
                  pl.BlockSpec((tm, kh), lambda i, j: (i, 0)),
                  pl.BlockSpec((kh, tn), lambda i, j: (0, j)),
                  pl.BlockSpec((kh, tn), lambda i, j: (1, j)),
                  pl.BlockSpec((tm, tn), lambda i, j: (i, j))],
        out_specs=pl.BlockSpec((tm, tn), lambda i, j: (i, j)),
        out_shape=jax.ShapeDtypeStruct((t, n), F32),
        compiler_params=_cparams("parallel", "arbitrary"),
        name="outproj",
    )(a1, a2, w, w, res)


def _ple_rows(h, g_ref, wg_ref, p_ref, wp_ref, gout_ref, norm_out):
    gate = _sigmoid(_dot(_rms(h, g_ref[...]).astype(BF16), wg_ref[...]))
    emb = _dot(p_ref[...].astype(BF16), wp_ref[...])
    y = h + gate * emb
    return _rms(y, gout_ref[...]) if norm_out else y


def _ple_kernel(h_ref, g_ref, wg_ref, p_ref, wp_ref, gout_ref, o_ref, *, norm_out):
    o_ref[...] = _ple_rows(h_ref[...], g_ref, wg_ref, p_ref, wp_ref, gout_ref, norm_out)


def ple_residual(h, g, wg, p, wp, g_out, *, norm_out, tm=512):
    t, d = h.shape
    pd = p.shape[1]
    tm = min(tm, t)
    return pl.pallas_call(
        functools.partial(_ple_kernel, norm_out=norm_out),
        grid=(t // tm,),
        in_specs=[pl.BlockSpec((tm, d), lambda i: (i, 0)),
                  pl.BlockSpec((1, d), lambda i: (0, 0)),
                  pl.BlockSpec((d, d), lambda i: (0, 0)),
                  pl.BlockSpec((tm, pd), lambda i: (i, 0)),
                  pl.BlockSpec((pd, d), lambda i: (0, 0)),
                  pl.BlockSpec((1, d), lambda i: (0, 0))],
        out_specs=pl.BlockSpec((tm, d), lambda i: (i, 0)),
        out_shape=jax.ShapeDtypeStruct((t, d), F32),
        compiler_params=_cparams("parallel"),
        name="ple",
    )(h, g.reshape(1, d).astype(F32), wg, p, wp, g_out.reshape(1, d).astype(F32))


def _rms(x, g):
    ms = jnp.mean(x * x, axis=-1, keepdims=True)
    return x * lax.rsqrt(ms + NORM_EPS) * g


def _rms_normed(h_ref, g_ref):
    return _rms(h_ref[...], g_ref[...])


def _route(xn, wr, br):
    n = xn.shape[0]
    x_hi, x_lo, _ = _split3(xn)
    w_hi, w_lo, _ = _split3(wr)
    logits = _dot(x_hi, w_hi) + (_dot(x_hi, w_lo) + _dot(x_lo, w_hi)) + br
    lane = lax.broadcasted_iota(I32, (n, LANE), 1)
    glog = jnp.where(lane < N_GROUPS, logits, NEG_INF)
    gmax = jnp.max(glog, axis=-1, keepdims=True)
    g_sel = jnp.min(jnp.where(glog == gmax, lane, LANE), axis=-1, keepdims=True)
    p_sel = 1.0 / jnp.sum(jnp.exp(glog - gmax), axis=-1, keepdims=True)
    eidx = lane - N_GROUPS
    in_grp = (eidx >= 0) & (eidx < N_EXPERTS) & ((eidx // EXPERTS_PER_GROUP) == g_sel)
    e1 = jnp.where(in_grp, logits, NEG_INF)
    v1 = jnp.max(e1, axis=-1, keepdims=True)
    i1 = jnp.min(jnp.where(e1 == v1, eidx, LANE), axis=-1, keepdims=True)
    e2 = jnp.where(eidx == i1, NEG_INF, e1)
    v2 = jnp.max(e2, axis=-1, keepdims=True)
    i2 = jnp.min(jnp.where(e2 == v2, eidx, LANE), axis=-1, keepdims=True)
    ex = jnp.exp(v2 - v1)
    return i1, i2, p_sel / (1.0 + ex), p_sel * ex / (1.0 + ex)


def _moe_kernel(h_ref, g_ref, wr_ref, br_ref, wgu_ref, wd_ref, o_ref, xn_ref, comb_ref, *, ke):
    e = pl.program_id(1)
    tm = h_ref.shape[0]

    @pl.when(e == 0)
    def _():
        xn = _rms_normed(h_ref, g_ref)
        xn_ref[...] = xn.astype(BF16)
        o_ref[...] = h_ref[...]
        i1, i2, w1, w2 = _route(xn, wr_ref[...], br_ref[...])
        lane = lax.broadcasted_iota(I32, (tm, LANE), 1)
        comb_ref[...] = jnp.where(lane == i1, w1, 0.0) + jnp.where(lane == i2, w2, 0.0)

    xn = xn_ref[...]
    lane = lax.broadcasted_iota(I32, (tm, LANE), 1)
    comb = comb_ref[...]
    hids = []
    for j in range(ke):
        gu = _dot(xn, wgu_ref[j])
        c = jnp.sum(jnp.where(lane == e * ke + j, comb, 0.0), axis=-1, keepdims=True)
        hid = _silu(gu[:, :EXPERT_FF]) * gu[:, EXPERT_FF:] * c
        hids.append(hid.astype(BF16))
    hid = jnp.concatenate(hids, axis=1)
    wd = wd_ref[...].reshape(ke * EXPERT_FF, wd_ref.shape[2])
    o_ref[...] += _dot(hid, wd)


def moe_residual(h, g, w_router, b_router, w_gu, w_down, *, tm=512, ke=4):
    t, d = h.shape
    tm = min(tm, t)
    return pl.pallas_call(
        functools.partial(_moe_kernel, ke=ke),
        grid=(t // tm, N_EXPERTS // ke),
        in_specs=[pl.BlockSpec((tm, d), lambda i, e: (i, 0)),
                  pl.BlockSpec((1, d), lambda i, e: (0, 0)),
                  pl.BlockSpec((d, LANE), lambda i, e: (0, 0)),
                  pl.BlockSpec((1, LANE), lambda i, e: (0, 0)),
                  pl.BlockSpec((ke, d, 2 * EXPERT_FF), lambda i, e: (e, 0, 0)),
                  pl.BlockSpec((ke, EXPERT_FF, d), lambda i, e: (e, 0, 0))],
        out_specs=pl.BlockSpec((tm, d), lambda i, e: (i, 0)),
        out_shape=jax.ShapeDtypeStruct((t, d), F32),
        scratch_shapes=[pltpu.VMEM((tm, d), BF16), pltpu.VMEM((tm, LANE), F32)],
        compiler_params=_cparams("parallel", "arbitrary"),
        name="moe",
    )(h, g.reshape(1, d).astype(F32), w_router, b_router, w_gu, w_down)


MOE_SUB = 256
MOE_CAP = 48
MOE_SLOTS = N_EXPERTS * MOE_CAP


def _moe_gather_kernel(h_ref, g_ref, wr_ref, br_ref, xc_ref, meta_ref, flag_ref):
    n = h_ref.shape[0]
    xn = _rms_normed(h_ref, g_ref)
    i1, i2, w1, w2 = _route(xn, wr_ref[...], br_ref[...])
    lane = lax.broadcasted_iota(I32, (n, LANE), 1)
    onehot = jnp.where((lane == i1) | (lane == i2), 1.0, 0.0)
    earlier = (lax.broadcasted_iota(I32, (n, n), 1) < lax.broadcasted_iota(I32, (n, n), 0)).astype(BF16)
    rank = _dot(earlier, onehot.astype(BF16))
    r1 = jnp.sum(jnp.where(lane == i1, rank, 0.0), axis=-1, keepdims=True)
    r2 = jnp.sum(jnp.where(lane == i2, rank, 0.0), axis=-1, keepdims=True)
    fits = (r1 < MOE_CAP) & (r2 < MOE_CAP)
    flag_ref[...] = jnp.broadcast_to(jnp.max(jnp.where(fits, 0.0, 1.0), axis=0, keepdims=True), flag_ref.shape)
    pos1 = jnp.where(r1 < MOE_CAP, i1.astype(F32) * MOE_CAP + r1, -1.0)
    pos2 = jnp.where(r2 < MOE_CAP, i2.astype(F32) * MOE_CAP + r2, -1.0)
    meta = (jnp.where(lane == 0, pos1, 0.0) + jnp.where(lane == 1, pos2, 0.0)
            + jnp.where(lane == 2, w1, 0.0) + jnp.where(lane == 3, w2, 0.0))
    meta_ref[...] = meta
    meta_t = meta.T
    slot = lax.broadcasted_iota(I32, (MOE_SLOTS, n), 0).astype(F32)
    pick = jnp.where((slot == meta_t[0:1, :]) | (slot == meta_t[1:2, :]), 1.0, 0.0).astype(BF16)
    xc = _dot(pick, xn.astype(BF16)).astype(BF16)
    xc_ref[...] = xc.reshape(xc_ref.shape)


def _moe_ffn_kernel(x_ref, wg_ref, wu_ref, wd_ref, y_ref):
    x = x_ref[...]
    hid = _silu(_dot(x, wg_ref[...].astype(BF16))) * _dot(x, wu_ref[...].astype(BF16))
    y_ref[...] = _dot(hid.astype(BF16), wd_ref[...].astype(BF16)).astype(y_ref.dtype)


def _moe_scatter_kernel(y_ref, meta_ref, h_ref, g_ref, wg_ref, p_ref, wp_ref, gout_ref, o_ref, *, norm_out):
    n = h_ref.shape[0]
    meta = meta_ref[...]
    slot = lax.broadcasted_iota(I32, (n, MOE_SLOTS), 1).astype(F32)
    y = y_ref[...].reshape(MOE_SLOTS, y_ref.shape[-1])
    pick = (jnp.where(slot == meta[:, 0:1], meta[:, 2:3], 0.0)
            + jnp.where(slot == meta[:, 1:2], meta[:, 3:4], 0.0)).astype(BF16)
    o_ref[...] = _ple_rows(h_ref[...] + _dot(pick, y), g_ref, wg_ref, p_ref, wp_ref, gout_ref, norm_out)


def moe_routed(h, g, w_router, b_router, w_gate, w_up, w_down, layer, ple, *, tm=1024):
    t, d = h.shape
    nsub = t // MOE_SUB
    rows = nsub * MOE_CAP
    tm = min(tm, rows)
    xc, meta, flag = pl.pallas_call(
        _moe_gather_kernel,
        grid=(nsub,),
        in_specs=[pl.BlockSpec((MOE_SUB, d), lambda i: (i, 0)),
                  pl.BlockSpec((1, d), lambda i: (0, 0)),
                  pl.BlockSpec((d, LANE), lambda i: (0, 0)),
                  pl.BlockSpec((1, LANE), lambda i: (0, 0))],
        out_specs=[pl.BlockSpec((N_EXPERTS, None, MOE_CAP, d), lambda i: (0, i, 0, 0)),
                   pl.BlockSpec((MOE_SUB, LANE), lambda i: (i, 0)),
                   pl.BlockSpec((None, 8, LANE), lambda i: (i, 0, 0))],
        out_shape=[jax.ShapeDtypeStruct((N_EXPERTS, nsub, MOE_CAP, d), BF16),
                   jax.ShapeDtypeStruct((t, LANE), F32),
                   jax.ShapeDtypeStruct((nsub, 8, LANE), F32)],
        compiler_params=_cparams("parallel"),
        name="moe_gather",
    )(h, g.reshape(1, d).astype(F32), w_router, b_router)
    y = pl.pallas_call(
        _moe_ffn_kernel,
        grid=(N_EXPERTS, rows // tm),
        in_specs=[pl.BlockSpec((None, tm, d), lambda e, i: (e, i, 0)),
                  pl.BlockSpec((None, None, d, EXPERT_FF), lambda e, i: (layer, e, 0, 0)),
                  pl.BlockSpec((None, None, d, EXPERT_FF), lambda e, i: (layer, e, 0, 0)),
                  pl.BlockSpec((None, None, EXPERT_FF, d), lambda e, i: (layer, e, 0, 0))],
        out_specs=pl.BlockSpec((None, tm, d), lambda e, i: (e, i, 0)),
        out_shape=jax.ShapeDtypeStruct((N_EXPERTS, rows, d), BF16),
        compiler_params=_cparams("parallel", "arbitrary"),
        name="moe_ffn",
    )(xc.reshape(N_EXPERTS, rows, d), w_gate, w_up, w_down)
    g_ple, wg, p, wp, g_out, norm_out = ple
    pd = p.shape[1]
    whole = lambda shape: pl.BlockSpec(shape, lambda i: (0, 0))
    out = pl.pallas_call(
        functools.partial(_moe_scatter_kernel, norm_out=norm_out),
        grid=(nsub,),
        in_specs=[pl.BlockSpec((N_EXPERTS, None, MOE_CAP, d), lambda i: (0, i, 0, 0)),
                  pl.BlockSpec((MOE_SUB, LANE), lambda i: (i, 0)),
                  pl.BlockSpec((MOE_SUB, d), lambda i: (i, 0)),
                  whole((1, d)), whole((d, d)), pl.BlockSpec((MOE_SUB, pd), lambda i: (i, 0)),
                  whole((pd, d)), whole((1, d))],
        out_specs=pl.BlockSpec((MOE_SUB, d), lambda i: (i, 0)),
        out_shape=jax.ShapeDtypeStruct((t, d), F32),
        compiler_params=_cparams("parallel"),
        name="moe_scatter_ple",
    )(y.reshape(N_EXPERTS, nsub, MOE_CAP, d), meta, h, g_ple.reshape(1, d).astype(F32), wg, p, wp,
      g_out.reshape(1, d).astype(F32))
    return out, jnp.max(flag)


def moe_ple_layer(h, g, w_router, b_router, w_gate, w_up, w_down, layer, ple):
    out, overflow = moe_routed(h, g, w_router, b_router, w_gate, w_up, w_down, layer, ple)

    def dense():
        w_gu = jnp.concatenate([w_gate[layer], w_up[layer]], axis=-1).astype(BF16)
        mid = moe_residual(h, g, w_router, b_router, w_gu, w_down[layer].astype(BF16))
        return ple_residual(mid, *ple[:-1], norm_out=ple[-1])

    return lax.cond(overflow > 0.0, dense, lambda: out)


def _dsa_prep_kernel(x_ref, ca_ref, sa_ref, ci_ref, slo_ref, shi_ref,
                     q_ref, k_ref, v_ref, iq_ref, ik_ref, iw_ref):
    ca, sa = ca_ref[...], sa_ref[...]
    ci, slo, shi = ci_ref[...], slo_ref[...], shi_ref[...]
    scale = A_HEAD_DIM ** -0.5 * math.log2(math.e)
    for h in range(A_HEADS):
        xh = x_ref[:, h * LANE:(h + 1) * LANE]
        q_ref[:, h * LANE:(h + 1) * LANE] = (_rope128(xh, ca, sa) * scale).astype(BF16)
    off = A_HEADS * A_HEAD_DIM
    for h in range(A_KV_HEADS):
        xh = x_ref[:, off + h * LANE:off + (h + 1) * LANE]
        k_ref[:, h * LANE:(h + 1) * LANE] = _rope128(xh, ca, sa).astype(BF16)
    off += A_KV_HEADS * A_HEAD_DIM
    v_ref[...] = x_ref[:, off:off + A_KV_HEADS * A_HEAD_DIM].astype(BF16)
    off += A_KV_HEADS * A_HEAD_DIM
    for h in range(IDX_HEADS * IDX_DIM // LANE):
        xh = x_ref[:, off + h * LANE:off + (h + 1) * LANE]
        iq_ref[:, h * LANE:(h + 1) * LANE] = _rope64(xh, ci, slo, shi).astype(BF16)
    off += IDX_HEADS * IDX_DIM
    misc = x_ref[:, off:off + LANE]
    ik_ref[...] = _rope64(misc, ci, slo, shi)[:, :IDX_DIM].astype(BF16)
    iw_ref[...] = misc[:, IDX_DIM:IDX_DIM + IDX_HEADS] * IDX_SCALE


def dsa_prep(proj, tabs, *, tm=512):
    t, n = proj.shape
    tm = min(tm, t)
    row = lambda w: pl.BlockSpec((tm, w), lambda i: (i, 0))
    return pl.pallas_call(
        _dsa_prep_kernel,
        grid=(t // tm,),
        in_specs=[row(n)] + [row(LANE)] * 5,
        out_specs=[row(1024), row(256), row(256), row(512), row(IDX_DIM), row(IDX_HEADS)],
        out_shape=[jax.ShapeDtypeStruct((t, 1024), BF16), jax.ShapeDtypeStruct((t, 256), BF16),
                   jax.ShapeDtypeStruct((t, 256), BF16), jax.ShapeDtypeStruct((t, 512), BF16),
                   jax.ShapeDtypeStruct((t, IDX_DIM), BF16), jax.ShapeDtypeStruct((t, IDX_HEADS), F32)],
        compiler_params=_cparams("parallel"),
        name="dsa_prep",
    )(proj, *tabs)


def _dsa_kernel(q_ref, k_ref, v_ref, iq_ref, ik_ref, iw_ref, o_ref,
                key_ref, m_ref, l_ref, acc_ref, *, tk, topk):
    qi = pl.program_id(1)
    nq = q_ref.shape[0]
    rep = A_HEADS // A_KV_HEADS
    n_tiles = ((qi + 1) * nq + tk - 1) // tk
    row = lax.broadcasted_iota(I32, (nq, tk), 0)
    col = lax.broadcasted_iota(I32, (nq, tk), 1)
    q_chunk = (qi * nq + row) // CHUNK

    def admissible(off):
        return ((off + col) // CHUNK) <= q_chunk

    iq = iq_ref[...]
    iw = iw_ref[...]

    def score_body(t, carry):
        off = pl.multiple_of(t * tk, tk)
        ikt = ik_ref[pl.ds(off, tk), :]
        s = jnp.zeros((nq, tk), F32)
        for h in range(IDX_HEADS):
            rel = _dot_nt(iq[:, h * IDX_DIM:(h + 1) * IDX_DIM], ikt)
            s = s + iw[:, h:h + 1] * jnp.maximum(rel, 0.0)
        key_ref[:, pl.ds(off, tk)] = jnp.where(admissible(off), s, NEG_INF)
        return carry

    lax.fori_loop(0, n_tiles, score_body, 0)

    def scan(fn, init):
        def body(t, acc):
            off = pl.multiple_of(t * tk, tk)
            blk = key_ref[:, pl.ds(off, tk)]
            for c in range(tk // LANE):
                acc = fn(blk[:, c * LANE:(c + 1) * LANE], acc)
            return acc
        return lax.fori_loop(0, n_tiles, body, init)

    def rowsum(x):
        return jnp.sum(x, axis=1, keepdims=True)

    def count_ge(v):
        return rowsum(scan(lambda blk, acc: acc + jnp.where(blk >= v, 1.0, 0.0), jnp.zeros((nq, LANE), F32)))

    big = jnp.full((nq, LANE), -NEG_INF, F32)
    zeros = jnp.zeros((nq, LANE), F32)
    lo_p, hi_p, n_p = scan(
        lambda blk, acc: (jnp.minimum(acc[0], jnp.where(blk > NEG_INF, blk, big)), jnp.maximum(acc[1], blk),
                          acc[2] + jnp.where(blk > NEG_INF, 1.0, 0.0)), (big, -big, zeros))
    enough = rowsum(n_p) >= topk
    lo0 = jnp.where(enough, jnp.min(lo_p, axis=1, keepdims=True), NEG_INF)
    hi0 = jnp.where(enough, jnp.max(hi_p, axis=1, keepdims=True), NEG_INF)

    def bisect_body(i, bracket):
        lo, hi = bracket
        mid = lo + 0.5 * (hi - lo)
        ge = count_ge(mid) >= topk
        return jnp.where(ge, mid, lo), jnp.where(ge, hi, mid)

    _, hi = lax.fori_loop(0, BISECT_STEPS, bisect_body, (lo0, hi0))

    def walk_cond(state):
        return state[1] > 0

    def walk_body(state):
        v, _ = state
        zero = jnp.zeros((nq, LANE), F32)
        cnt_p, nxt_p = scan(lambda blk, acc: (acc[0] + jnp.where(blk >= v, 1.0, 0.0),
                                              jnp.maximum(acc[1], jnp.where(blk < v, blk, -big))),
                            (zero, -big))
        found = rowsum(cnt_p) >= topk
        v = jnp.where(found, v, jnp.max(nxt_p, axis=1, keepdims=True))
        return v, jnp.sum(jnp.where(found, 0, 1))

    hi_val = jnp.max(scan(lambda blk, acc: jnp.maximum(acc, jnp.where(blk <= hi, blk, -big)), -big),
                     axis=1, keepdims=True)
    thr, _ = lax.while_loop(walk_cond, walk_body, (hi_val, jnp.int32(1)))
    n_gt = rowsum(scan(lambda blk, acc: acc + jnp.where(blk > thr, 1.0, 0.0), jnp.zeros((nq, LANE), F32)))
    need = topk - n_gt

    m_ref[...] = jnp.full(m_ref.shape, NEG_INF, F32)
    l_ref[...] = jnp.zeros(l_ref.shape, F32)
    acc_ref[...] = jnp.zeros(acc_ref.shape, F32)
    upper = (lax.broadcasted_iota(I32, (LANE, LANE), 0)
             < lax.broadcasted_iota(I32, (LANE, LANE), 1)).astype(BF16)
    qg = [jnp.concatenate([q_ref[:, (g * rep + r) * LANE:(g * rep + r + 1) * LANE]
                           for r in range(rep)], axis=0) for g in range(A_KV_HEADS)]

    def attn_body(t, run):
        off = pl.multiple_of(t * tk, tk)
        keyt = key_ref[:, pl.ds(off, tk)]
        adm = admissible(off)
        sels = []
        for c in range(tk // LANE):
            kc = keyt[:, c * LANE:(c + 1) * LANE]
            eqf = jnp.where(kc == thr, 1.0, 0.0)
            before = _dot(eqf.astype(BF16), upper) + run
            take = jnp.where(kc > thr, 1.0, jnp.where(before < need, eqf, 0.0))
            sels.append(take)
            run = run + jnp.sum(eqf, axis=1, keepdims=True)
        sel = jnp.where(adm, jnp.concatenate(sels, axis=1), 0.0) > 0.5
        sel_r = jnp.concatenate([sel] * rep, axis=0)
        groups = range(A_KV_HEADS)
        ss = [jnp.where(sel_r, _dot_nt(qg[g], k_ref[pl.ds(off, tk), g * LANE:(g + 1) * LANE]), NEG_INF)
              for g in groups]
        m_old = [m_ref[g] for g in groups]
        m_new = [jnp.maximum(m_old[g], _rowmax(ss[g])) for g in groups]
        ps = [jnp.exp2(ss[g] - m_new[g]) for g in groups]
        pv = [_dot(ps[g].astype(BF16), v_ref[pl.ds(off, tk), g * LANE:(g + 1) * LANE]) for g in groups]
        for g in groups:
            alpha = jnp.exp2(m_old[g] - m_new[g])
            l_ref[g] = alpha * l_ref[g] + _rowsum(ps[g])
            acc_ref[g] = alpha * acc_ref[g] + pv[g]
            m_ref[g] = m_new[g]
        return run

    lax.fori_loop(0, n_tiles, attn_body, jnp.zeros((nq, 1), F32))
    for g in range(A_KV_HEADS):
        out = acc_ref[g] / l_ref[g]
        for r in range(rep):
            h = g * rep + r
            o_ref[:, h * LANE:(h + 1) * LANE] = out[r * nq:(r + 1) * nq].astype(o_ref.dtype)


def dsa_attention(q, k, v, iq, ik, iw, *, nq=128, tk=1024):
    b, s, _ = q.shape
    tk = min(tk, s)
    topk = min(TOPK_MAX, s // 4)
    rep = A_HEADS // A_KV_HEADS
    qspec = lambda w: pl.BlockSpec((None, nq, w), lambda bi, qi: (bi, qi, 0))
    kspec = lambda w: pl.BlockSpec((None, s, w), lambda bi, qi: (bi, 0, 0))
    return pl.pallas_call(
        functools.partial(_dsa_kernel, tk=tk, topk=topk),
        grid=(b, s // nq),
        in_specs=[qspec(1024), kspec(256), kspec(256), qspec(512), kspec(IDX_DIM), qspec(IDX_HEADS)],
        out_specs=qspec(1024),
        out_shape=jax.ShapeDtypeStruct((b, s, 1024), BF16),
        scratch_shapes=[pltpu.VMEM((nq, s), F32),
                        pltpu.VMEM((A_KV_HEADS, rep * nq, 1), F32),
                        pltpu.VMEM((A_KV_HEADS, rep * nq, 1), F32),
                        pltpu.VMEM((A_KV_HEADS, rep * nq, LANE), F32)],
        compiler_params=_cparams("parallel", "arbitrary"),
        name="dsa_attention",
    )(q, k, v, iq, ik, iw)


def _causal_conv(x, tail_ref, xp_ref, w):
    r = x.shape[0]
    width = w.shape[0]
    xp_ref[0:8, :] = tail_ref[...]
    xp_ref[8:8 + r, :] = x
    tail_ref[...] = x[r - 8:r, :]
    acc = x * w[width - 1:width, :]
    for j in range(width - 1):
        acc = acc + xp_ref[pl.ds(8 - (width - 1) + j, r), :] * w[j:j + 1, :]
    return acc


def _unit_lower_inverse(mats):
    n = mats[0].shape[0]
    eye = (lax.broadcasted_iota(I32, (n, n), 0) == lax.broadcasted_iota(I32, (n, n), 1)).astype(F32)
    ps = [-a for a in mats]
    ts = [eye + p for p in ps]
    for _ in range(int(math.ceil(math.log2(n))) - 1):
        ps = [_bdot(p, p) for p in ps]
        ts = [t + _bdot(t, p) for t, p in zip(ts, ps)]
    return ts


def _gdn_kernel(qkv_ref, z_ref, misc_ref, convw_ref, alog_ref, dtb_ref, nw_ref, o_ref,
                tail_ref, xp_ref, q_s, k_s, v_s, beta_s, g_s, state_ref):
    nb, rows = qkv_ref.shape[0], qkv_ref.shape[1]
    hd = B_HEAD_DIM
    nh = B_HEADS

    @pl.when(pl.program_id(1) == 0)
    def _():
        tail_ref[...] = jnp.zeros(tail_ref.shape, F32)
        state_ref[...] = jnp.zeros(state_ref.shape, F32)

    for b in range(nb):
        y = _silu(_causal_conv(qkv_ref[b], tail_ref.at[b], xp_ref.at[b], convw_ref[...]))
        for h in range(nh):
            qh = y[:, h * hd:(h + 1) * hd]
            kh = y[:, (nh + h) * hd:(nh + h + 1) * hd]
            q_s[b, :, h * hd:(h + 1) * hd] = (qh * lax.rsqrt(jnp.sum(qh * qh, -1, keepdims=True) + 1e-6)
                                              * hd ** -0.5)
            k_s[b, :, h * hd:(h + 1) * hd] = kh * lax.rsqrt(jnp.sum(kh * kh, -1, keepdims=True) + 1e-6)
        v_s[b] = y[:, 2 * nh * hd:]
        misc = misc_ref[b]
        beta_s[b] = _sigmoid(misc)
        g_s[b] = -jnp.exp(alog_ref[...]) * _softplus(misc + dtb_ref[...])

    ri = lax.broadcasted_iota(I32, (CHUNK, CHUNK), 0)
    ci = lax.broadcasted_iota(I32, (CHUNK, CHUNK), 1)
    incl = ri >= ci
    strict = ri > ci
    lower01 = incl.astype(BF16)
    nw = nw_ref[...]

    def chunk_body(c, carry):
        r0 = pl.multiple_of(c * CHUNK, CHUNK)
        rs = pl.ds(r0, CHUNK)
        gc = [_dot01_left(lower01, g_s[b, rs, :]) for b in range(nb)]
        gct = [x.T for x in gc]
        beta = [beta_s[b, rs, :] for b in range(nb)]
        eg = [jnp.exp(x) for x in gc]
        e_last = [jnp.exp(x[CHUNK - 1:CHUNK, :]) for x in gc]
        e_rem = [jnp.exp(x[CHUNK - 1:CHUNK, :] - x) for x in gc]
        items = [(b, h) for b in range(nb) for h in range(nh)]
        n = range(len(items))
        hs = [slice(h * hd, (h + 1) * hd) for _, h in items]
        col = lambda x, i: x[items[i][0]][:, nh + items[i][1]:nh + items[i][1] + 1]
        qh = [q_s[items[i][0], rs, hs[i]] for i in n]
        kh = [k_s[items[i][0], rs, hs[i]] for i in n]
        bcol = [beta[b][:, h:h + 1] for b, h in items]
        kb = [kh[i] * bcol[i] for i in n]
        kk = [_bdot_nt(kb[i], kh[i]) for i in n]
        qk = [_bdot_nt(qh[i], kh[i]) for i in n]
        st = [state_ref[i] for i in n]
        q_st = [_bdot(qh[i] * col(eg, i), st[i]) for i in n]
        decay = []
        for i, (b, h) in enumerate(items):
            diff = col(gc, i) - gct[b][nh + h:nh + h + 1, :]
            decay.append(jnp.where(incl, jnp.exp(jnp.where(incl, diff, 0.0)), 0.0))
        t_mat = _unit_lower_inverse([jnp.where(strict, kk[i] * decay[i], 0.0) for i in n])
        uw = [_bdot(t_mat[i], jnp.concatenate([v_s[items[i][0], rs, hs[i]] * bcol[i], kb[i] * col(eg, i)],
                                              axis=1)) for i in n]
        w_st = [_bdot(uw[i][:, hd:], st[i]) for i in n]
        v_new = [uw[i][:, :hd] - w_st[i] for i in n]
        o = [q_st[i] + _bdot(qk[i] * decay[i], v_new[i]) for i in n]
        upd = [_bdot((kh[i] * col(e_rem, i)).T, v_new[i]) for i in n]
        for i, (b, h) in enumerate(items):
            state_ref[i] = st[i] * col(e_last, i) + upd[i]
            on = o[i] * lax.rsqrt(jnp.mean(o[i] * o[i], -1, keepdims=True) + NORM_EPS) * nw
            o_ref[b, rs, hs[i]] = (on * _silu(z_ref[b, rs, hs[i]])).astype(o_ref.dtype)
        return carry

    lax.fori_loop(0, rows // CHUNK, chunk_body, 0)


def gdn_mixer(proj, conv_w, alog_row, dtb_row, norm_w, *, batch, rows=256, nb=2):
    t = proj.shape[0]
    s = t // batch
    rows = min(rows, s)
    nb = math.gcd(nb, batch)
    nqkv = 3 * B_HEADS * B_HEAD_DIM
    nz = B_HEADS * B_HEAD_DIM
    proj = proj.reshape(batch, s, proj.shape[1])
    const = lambda shape: pl.BlockSpec(shape, lambda b, r: (0, 0))
    out = pl.pallas_call(
        _gdn_kernel,
        grid=(batch // nb, s // rows),
        in_specs=[pl.BlockSpec((nb, rows, nqkv), lambda b, r: (b, r, 0)),
                  pl.BlockSpec((nb, rows, nz), lambda b, r: (b, r, nqkv // nz)),
                  pl.BlockSpec((nb, rows, LANE), lambda b, r: (b, r, (nqkv + nz) // LANE)),
                  const((CONV_WIDTH, nqkv)), const((1, LANE)), const((1, LANE)), const((1, LANE))],
        out_specs=pl.BlockSpec((nb, rows, nz), lambda b, r: (b, r, 0)),
        out_shape=jax.ShapeDtypeStruct((batch, s, nz), BF16),
        scratch_shapes=[pltpu.VMEM((nb, 8, nqkv), F32), pltpu.VMEM((nb, rows + 8, nqkv), F32),
                        pltpu.VMEM((nb, rows, nz), F32), pltpu.VMEM((nb, rows, nz), F32),
                        pltpu.VMEM((nb, rows, nz), F32),
                        pltpu.VMEM((nb, rows, LANE), F32), pltpu.VMEM((nb, rows, LANE), F32),
                        pltpu.VMEM((nb * B_HEADS, B_HEAD_DIM, B_HEAD_DIM), F32)],
        compiler_params=_cparams("parallel", "arbitrary"),
        name="gdn",
    )(proj, proj, proj, conv_w, alog_row, dtb_row, norm_w)
    return out.reshape(t, nz)


def _mla_prep_kernel(q_ref, kr_ref, ci_ref, slo_ref, shi_ref, qo_ref, kro_ref):
    ci, slo, shi = ci_ref[...], slo_ref[...], shi_ref[...]
    scale = (C_NOPE + C_ROPE) ** -0.5 * math.log2(math.e)
    for h in range(C_HEADS):
        base = h * 2 * LANE
        qo_ref[:, base:base + LANE] = (q_ref[:, base:base + LANE] * scale).astype(BF16)
        qo_ref[:, base + LANE:base + 2 * LANE] = (
            _rope64(q_ref[:, base + LANE:base + 2 * LANE], ci, slo, shi) * scale).astype(BF16)
    kro_ref[...] = _rope64(kr_ref[...], ci, slo, shi).astype(BF16)


def mla_prep(q_raw, proj_c, tabs, *, tm=512):
    t, n = q_raw.shape
    tm = min(tm, t)
    row = lambda w: pl.BlockSpec((tm, w), lambda i: (i, 0))
    return pl.pallas_call(
        _mla_prep_kernel,
        grid=(t // tm,),
        in_specs=[row(n), pl.BlockSpec((tm, LANE), lambda i: (i, (C_Q_RANK + C_KV_RANK) // LANE)),
                  row(LANE), row(LANE), row(LANE)],
        out_specs=[row(n), row(LANE)],
        out_shape=[jax.ShapeDtypeStruct((t, n), BF16), jax.ShapeDtypeStruct((t, LANE), BF16)],
        compiler_params=_cparams("parallel"),
        name="mla_prep",
    )(q_raw, proj_c, *tabs)


def _mla_kernel(q_ref, kv_ref, kr_ref, o_ref, m_ref, l_ref, acc_ref, *, hb):
    qi = pl.program_id(2)
    tq = q_ref.shape[0]
    heads = range(hb)
    row = lax.broadcasted_iota(I32, (tq, tq), 0)
    col = lax.broadcasted_iota(I32, (tq, tq), 1)
    diag_mask = (col // CHUNK) <= (row // CHUNK)
    m_ref[...] = jnp.full(m_ref.shape, NEG_INF, F32)
    l_ref[...] = jnp.zeros(l_ref.shape, F32)
    acc_ref[...] = jnp.zeros(acc_ref.shape, F32)
    qs = [q_ref[:, h * 2 * LANE:(h + 1) * 2 * LANE] for h in heads]

    def step(off, on_diagonal):
        rows = pl.ds(off, tq)
        kr = kr_ref[rows, :]
        def scores(h):
            s = _dot_nt(qs[h], jnp.concatenate([kv_ref[rows, h * 2 * LANE:h * 2 * LANE + LANE], kr], axis=1))
            return jnp.where(diag_mask, s, NEG_INF) if on_diagonal else s

        def update(h, s):
            m_old = m_ref[h]
            m_new = jnp.maximum(m_old, _rowmax(s))
            p = jnp.exp2(s - m_new)
            pv = _dot(p.astype(BF16), kv_ref[rows, h * 2 * LANE + LANE:(h + 1) * 2 * LANE])
            alpha = jnp.exp2(m_old - m_new)
            l_ref[h] = alpha * l_ref[h] + _rowsum(p)
            acc_ref[h] = alpha * acc_ref[h] + pv
            m_ref[h] = m_new

        s_next = scores(0)
        for h in heads:
            s_cur = s_next
            if h + 1 < hb:
                s_next = scores(h + 1)
            update(h, s_cur)

    def body(t, carry):
        step(pl.multiple_of(t * tq, tq), False)
        return carry

    lax.fori_loop(0, qi, body, 0)
    step(pl.multiple_of(qi * tq, tq), True)
    for h in heads:
        o_ref[:, h * LANE:(h + 1) * LANE] = (acc_ref[h] / l_ref[h]).astype(o_ref.dtype)


def mla_attention(q, kv, kr, *, tq=1024, hb=2):
    b, s, _ = q.shape
    tq = min(tq, s)
    return pl.pallas_call(
        functools.partial(_mla_kernel, hb=hb),
        grid=(b, C_HEADS // hb, s // tq),
        in_specs=[pl.BlockSpec((None, tq, hb * 2 * LANE), lambda bi, h, qi: (bi, qi, h)),
                  pl.BlockSpec((None, s, hb * 2 * LANE), lambda bi, h, qi: (bi, 0, h)),
                  pl.BlockSpec((None, s, LANE), lambda bi, h, qi: (bi, 0, 0))],
        out_specs=pl.BlockSpec((None, tq, hb * LANE), lambda bi, h, qi: (bi, qi, h)),
        out_shape=jax.ShapeDtypeStruct((b, s, C_HEADS * C_V), BF16),
        scratch_shapes=[pltpu.VMEM((hb, tq, 1), F32), pltpu.VMEM((hb, tq, 1), F32),
                        pltpu.VMEM((hb, tq, LANE), F32)],
        compiler_params=_cparams("parallel", "parallel", "arbitrary"),
        name="mla_attention",
    )(q, kv, kr)


def _ssd_kernel(zx_ref, misc_ref, convw_ref, convb_ref, dtb_ref, a_ref, aexp_ref, dskip_ref,
                nw_ref, expand_ref, o_ref,
                tail_ref, xp_ref, xs_s, xdt_s, bm_s, cm_s, la_s, laexp_s, y_s, state_ref):
    rows = zx_ref.shape[0]
    pd = D_HEAD_DIM
    gw = D_INNER // D_GROUPS
    pairs_per_group = gw // LANE

    @pl.when(pl.program_id(1) == 0)
    def _():
        tail_ref[...] = jnp.zeros(tail_ref.shape, F32)
        state_ref[...] = jnp.zeros(state_ref.shape, F32)

    y = _silu(_causal_conv(zx_ref[:, D_INNER:], tail_ref, xp_ref, convw_ref[...]) + convb_ref[...])
    xs = y[:, :D_INNER]
    xs_s[...] = xs
    bm_s[...] = y[:, D_INNER:D_INNER + D_GROUPS * D_STATE]
    cm_s[...] = y[:, D_INNER + D_GROUPS * D_STATE:]
    dt = _softplus(misc_ref[...] + dtb_ref[...])
    dt_exp = _dot01_right(dt, expand_ref[...])
    xdt_s[...] = xs * dt_exp
    la_s[...] = dt * a_ref[...]
    laexp_s[...] = dt_exp * aexp_ref[...]

    ri = lax.broadcasted_iota(I32, (CHUNK, CHUNK), 0)
    ci = lax.broadcasted_iota(I32, (CHUNK, CHUNK), 1)
    incl = ri >= ci
    lower01 = incl.astype(BF16)
    lane_lo = lax.broadcasted_iota(I32, (CHUNK, LANE), 1) < pd

    def chunk_body(c, carry):
        r0 = pl.multiple_of(c * CHUNK, CHUNK)
        rs = pl.ds(r0, CHUNK)
        cum = _dot01_left(lower01, la_s[rs, :])
        cumt = cum.T
        cum_e = _dot01_left(lower01, laexp_s[rs, :])
        ecum = jnp.exp(cum_e)
        last = cum_e[CHUNK - 1:CHUNK, :]
        dec_st = jnp.exp(last - cum_e)
        cdec = jnp.exp(last)
        for g in range(D_GROUPS):
            bm = bm_s[rs, g * D_STATE:(g + 1) * D_STATE]
            cm = cm_s[rs, g * D_STATE:(g + 1) * D_STATE]
            cb = _bdot_nt(cm, bm)
            bmt = bm.T.astype(BF16)
            gs = slice(g * gw, (g + 1) * gw)
            y_off = _bdot(cm, state_ref[:, gs]) * ecum[:, gs]
            for pp in range(pairs_per_group):
                pidx = g * pairs_per_group + pp
                ps = slice(pidx * LANE, (pidx + 1) * LANE)
                xp = xdt_s[rs, ps]
                ys = []
                for hh in range(2):
                    h = 2 * pidx + hh
                    diff = cum[:, h:h + 1] - cumt[h:h + 1, :]
                    lmat = jnp.where(incl, jnp.exp(jnp.where(incl, diff, 0.0)), 0.0)
                    ys.append(_bdot(cb * lmat, xp))
                y_diag = jnp.where(lane_lo, ys[0], ys[1])
                y_s[rs, ps] = (y_diag + y_off[:, pp * LANE:(pp + 1) * LANE]
                               + dskip_ref[:, ps] * xs_s[rs, ps])
                s_new = _dot(bmt, (xp * dec_st[:, ps]).astype(BF16))
                state_ref[:, ps] = state_ref[:, ps] * cdec[:, ps] + s_new
        return carry

    lax.fori_loop(0, rows // CHUNK, chunk_body, 0)
    yv = y_s[...] * _silu(zx_ref[:, :D_INNER])
    for g in range(D_GROUPS):
        gs = slice(g * gw, (g + 1) * gw)
        yg = yv[:, gs]
        ms = jnp.mean(yg * yg, axis=-1, keepdims=True)
        o_ref[:, gs] = (yg * lax.rsqrt(ms + NORM_EPS) * nw_ref[:, gs]).astype(o_ref.dtype)


def ssd_mixer(proj, conv_w, conv_b, dtb_row, a_row, a_exp, dskip_exp, norm_w, expand, *, batch, rows=256):
    t = proj.shape[0]
    s = t // batch
    rows = min(rows, s)
    nr = s // rows
    nx = D_INNER + 2 * D_GROUPS * D_STATE
    const = lambda shape: pl.BlockSpec(shape, lambda b, r: (0, 0))
    return pl.pallas_call(
        _ssd_kernel,
        grid=(batch, nr),
        in_specs=[pl.BlockSpec((rows, D_INNER + nx), lambda b, r: (b * nr + r, 0)),
                  pl.BlockSpec((rows, LANE), lambda b, r: (b * nr + r, (D_INNER + nx) // LANE)),
                  const((CONV_WIDTH, nx)), const((1, nx)), const((1, LANE)), const((1, LANE)),
                  const((1, D_INNER)), const((1, D_INNER)), const((1, D_INNER)), const((LANE, D_INNER))],
        out_specs=pl.BlockSpec((rows, D_INNER), lambda b, r: (b * nr + r, 0)),
        out_shape=jax.ShapeDtypeStruct((t, D_INNER), BF16),
        scratch_shapes=[pltpu.VMEM((8, nx), F32), pltpu.VMEM((rows + 8, nx), F32),
                        pltpu.VMEM((rows, D_INNER), F32), pltpu.VMEM((rows, D_INNER), F32),
                        pltpu.VMEM((rows, D_GROUPS * D_STATE), F32), pltpu.VMEM((rows, D_GROUPS * D_STATE), F32),
                        pltpu.VMEM((rows, LANE), F32), pltpu.VMEM((rows, D_INNER), F32),
                        pltpu.VMEM((rows, D_INNER), F32),
                        pltpu.VMEM((D_STATE, D_INNER), F32)],
        compiler_params=_cparams("parallel", "arbitrary"),
        name="ssd",
    )(proj, proj, conv_w, conv_b, dtb_row, a_row, a_exp, dskip_exp, norm_w, expand)


def _rope_tables(positions):
    pos = positions.reshape(-1).astype(F32)[:, None]

    def cs(dim):
        inv = jnp.power(ROPE_THETA, -jnp.arange(0, dim, 2, dtype=F32) / dim)
        ang = pos * inv
        return jnp.cos(ang), jnp.sin(ang)

    c, s = cs(A_HEAD_DIM)
    tab_a = (jnp.concatenate([c, c], -1), jnp.concatenate([-s, s], -1))
    c, s = cs(IDX_DIM)
    z = jnp.zeros_like(s)
    tab_i = (jnp.concatenate([c, c, c, c], -1), jnp.concatenate([-s, z, -s, z], -1),
             jnp.concatenate([z, s, z, s], -1))
    return tab_a, tab_i


def _pad_cols(w, n):
    return jnp.pad(w, ((0, 0), (0, n - w.shape[1])))


def _lane_row(v, start):
    return jnp.zeros((1, LANE), F32).at[0, start:start + v.shape[0]].set(v.astype(F32))


def _moe_weights(i, w_group, b_group, w_expert, b_expert, w_gate, w_up, w_down):
    w_router = _pad_cols(jnp.concatenate([w_group[i], w_expert[i]], axis=1), LANE).astype(F32)
    b_router = _pad_cols(jnp.concatenate([b_group[i], b_expert[i]])[None], LANE).astype(F32)
    return w_router, b_router, w_gate, w_up, w_down, i


def kernel(x, p, positions, norm_mix, norm_ffn, norm_ple, norm_final, ev_w_in, ev_w_out, gdn_conv_w,
           gdn_a_log, gdn_dt_bias, gdn_norm, od_w_in, od_w_out, mla_q_norm, mla_kv_norm, mla_w_uq,
           mla_w_ukv, ssm_conv_w, ssm_conv_b, ssm_a_log, ssm_dt_bias, ssm_d_skip, ssm_norm, moe_w_group,
           moe_b_group, moe_w_expert, moe_b_expert, moe_w_gate, moe_w_up, moe_w_down, ple_w_proj,
           ple_w_gate):
    b, s, d = x.shape
    t = b * s
    tab_a, tab_i = _rope_tables(positions)
    h = x.reshape(t, d)
    moe_args = (moe_w_group, moe_b_group, moe_w_expert, moe_b_expert, moe_w_gate, moe_w_up, moe_w_down)

    w = ev_w_in[0]
    o = np_cumsum((1024, 256, 256, 512, 64, 8, 3072, 1024, 8, 8))
    w_a = _pad_cols(w[:, :o[6]], 2176).astype(BF16)
    w_b = _pad_cols(jnp.concatenate([w[:, o[6]:o[8]], w[:, o[8]:]], axis=1), 4224).astype(BF16)
    proj_a = norm_matmul(h, norm_mix[0], w_a, k=d)
    proj_b = norm_matmul(h, norm_mix[0], w_b, k=d, tn=1408)
    q, k, v, iq, ik, iw = dsa_prep(proj_a, tab_a + tab_i)
    r3 = lambda a: a.reshape(b, s, a.shape[-1])
    o_a = dsa_attention(r3(q), r3(k), r3(v), r3(iq), r3(ik), r3(iw)).reshape(t, -1)
    o_b = gdn_mixer(proj_b, gdn_conv_w[0].astype(F32), _lane_row(gdn_a_log[0], B_HEADS),
                    _lane_row(gdn_dt_bias[0], B_HEADS), gdn_norm[0].reshape(1, -1).astype(F32), batch=b)
    h = outproj_residual(o_a, o_b, ev_w_out[0].astype(BF16), h)
    h = moe_ple_layer(h, norm_ffn[0], *_moe_weights(0, *moe_args),
                      (norm_ple[0], ple_w_gate[0].astype(BF16), p[0].reshape(t, -1),
                       ple_w_proj[0].astype(BF16), norm_final, False))

    w = od_w_in[0]
    o = np_cumsum((512, 512, 64, 1024, 1536, 16))
    w_c = _pad_cols(w[:, :o[3]], 1152).astype(BF16)
    w_d = _pad_cols(w[:, o[3]:], 2688).astype(BF16)
    proj_c = norm_matmul(h, norm_mix[1], w_c, k=d)
    proj_d = norm_matmul(h, norm_mix[1], w_d, k=d, tn=896)
    w_uq = jnp.pad(mla_w_uq[0].reshape(C_Q_RANK, C_HEADS, C_NOPE + C_ROPE),
                   ((0, 0), (0, 0), (0, 2 * LANE - C_NOPE - C_ROPE))).reshape(C_Q_RANK, -1).astype(BF16)
    q_raw = norm_matmul(proj_c, mla_q_norm[0], w_uq, k=C_Q_RANK, xblk=0, tn=1024)
    kv = norm_matmul(proj_c, mla_kv_norm[0], mla_w_ukv[0].astype(BF16), k=C_KV_RANK, xblk=1, tn=1024,
                     out_dtype=BF16)
    q, kr = mla_prep(q_raw, proj_c, tab_i)
    o_c = mla_attention(r3(q), r3(kv), r3(kr)).reshape(t, -1)
    a_heads = -jnp.exp(ssm_a_log[0].astype(F32))
    expand = (jnp.arange(LANE)[:, None] == (jnp.arange(D_INNER)[None, :] // D_HEAD_DIM)).astype(BF16)
    o_d = ssd_mixer(proj_d, ssm_conv_w[0].astype(F32), ssm_conv_b[0].reshape(1, -1).astype(F32),
                    _lane_row(ssm_dt_bias[0], 0), _lane_row(a_heads, 0),
                    jnp.repeat(a_heads, D_HEAD_DIM)[None], jnp.repeat(ssm_d_skip[0].astype(F32), D_HEAD_DIM)[None],
                    ssm_norm[0].reshape(1, -1).astype(F32), expand, batch=b)
    h = outproj_residual(o_c, o_d, od_w_out[0].astype(BF16), h)
    out = moe_ple_layer(h, norm_ffn[1], *_moe_weights(1, *moe_args),
                        (norm_ple[1], ple_w_gate[1].astype(BF16), p[1].reshape(t, -1),
                         ple_w_proj[1].astype(BF16), norm_final, True))
    return out.reshape(b, s, d)


def np_cumsum(sizes):
    out, acc = [], 0
    for v in sizes:
        out.append(acc)
        acc += v
    return out
```
